```python
import jax
import jax.numpy as jnp
from jax import lax
import numpy as np

D_MODEL = 1024
BATCH = 8
SEQ = 2048
DEPTH = 4
DEC_BATCH = 4
DEC_SEQ = 4096
PAST_LEN = 128

GRID_W = 64
N_MIXERS = 2
EXPAND = 2
BRANCH_WIDTH = EXPAND * D_MODEL
SSD_HEAD_DIM = 64
SSD_HEADS = BRANCH_WIDTH // SSD_HEAD_DIM
SSD_GROUPS = 4
SSD_STATE = 128
SSD_CONV = 5
SSD_CHUNK = 128
SSD_CONV_DIM = BRANCH_WIDTH + 2 * SSD_GROUPS * SSD_STATE
NA_HEAD_DIM = 64
NA_HEADS = BRANCH_WIDTH // NA_HEAD_DIM
NA_KH = 8
NA_KW = 16
MEM_TOKENS = 256
MEM_HEADS = 4
MEM_HEAD_DIM = D_MODEL // MEM_HEADS
MEM_WIDTH = MEM_HEADS * MEM_HEAD_DIM
MIX_WIDTH = BRANCH_WIDTH + MEM_WIDTH
SSD_IN = MIX_WIDTH + SSD_CONV_DIM + 2 * SSD_HEADS + MEM_WIDTH
NA_IN = MIX_WIDTH + 3 * BRANCH_WIDTH + MEM_WIDTH
N_SSD_LAYERS = (DEPTH + 1) // 2
N_NA_LAYERS = DEPTH // 2
EPS = 1e-6

kernel_name = 'hybrid_ssd_natten_memory_encoder'


def rmsnorm(x, g):
    xf = x.astype(jnp.float32)
    y = xf * lax.rsqrt(jnp.mean(xf * xf, axis=-1, keepdims=True) + EPS)
    return (y * g.astype(jnp.float32)).astype(x.dtype)


def depthwise_conv_centred(u, w, b):
    pad = SSD_CONV // 2
    y = lax.conv_general_dilated(u, w[:, None, :].astype(u.dtype), window_strides=(1,),
                                 padding=[(pad, pad)], dimension_numbers=('NWC', 'WIO', 'NWC'),
                                 feature_group_count=u.shape[-1])
    return y + b.astype(u.dtype)


def ssd_chunked(x, a_dt, bm, cm):
    b, t, _, p = x.shape
    c, l = t // SSD_CHUNK, SSD_CHUNK
    r = SSD_HEADS // SSD_GROUPS
    x = x.reshape(b, c, l, SSD_GROUPS, r, p)
    bm = bm.reshape(b, c, l, SSD_GROUPS, SSD_STATE)
    cm = cm.reshape(b, c, l, SSD_GROUPS, SSD_STATE)
    a = a_dt.reshape(b, c, l, SSD_GROUPS, r).transpose(0, 1, 3, 4, 2)
    a_cs = jnp.cumsum(a, axis=-1)
    lower = jnp.tril(jnp.ones((l, l), dtype=bool))
    seg = a_cs[..., :, None] - a_cs[..., None, :]
    decay_ls = jnp.exp(jnp.where(lower, seg, -jnp.inf))
    cb = jnp.einsum('bclgn,bcsgn->bcgls', cm, bm)
    y_diag = jnp.einsum('bcgrls,bcsgrp->bclgrp', cb[:, :, :, None] * decay_ls, x)
    decay_to_end = jnp.exp(a_cs[..., -1:] - a_cs)
    states = jnp.einsum('bclgn,bcgrl,bclgrp->bcgrpn', bm, decay_to_end, x)
    chunk_decay = jnp.exp(a_cs[..., -1])

    def step(carry, inp):
        st, dec = inp
        return carry * dec[..., None, None] + st, carry

    init = jnp.zeros(states.shape[:1] + states.shape[2:], states.dtype)
    _, prev = lax.scan(step, init, (states.swapaxes(0, 1), chunk_decay.swapaxes(0, 1)))
    prev = prev.swapaxes(0, 1)
    y_off = jnp.einsum('bclgn,bcgrpn,bcgrl->bclgrp', cm, prev, jnp.exp(a_cs))
    return (y_diag + y_off).reshape(b, t, SSD_HEADS, p)


def ssd_direction(xs, bm, cm, dt_raw, dt_bias, a_log):
    dt = jax.nn.softplus(dt_raw.astype(jnp.float32) + dt_bias.astype(jnp.float32))
    a_dt = -jnp.exp(a_log.astype(jnp.float32)) * dt
    return ssd_chunked(xs * dt[..., None].astype(xs.dtype), a_dt, bm, cm)


def memory_attention(q_mem, mem_n, w_mem_kv):
    b, t, _ = q_mem.shape
    m = mem_n.shape[1]
    q = q_mem.reshape(b, t, MEM_HEADS, MEM_HEAD_DIM) * (MEM_HEAD_DIM ** -0.5)
    kv = (mem_n @ w_mem_kv).reshape(b, m, 2, MEM_HEADS, MEM_HEAD_DIM)
    k, v = kv[:, :, 0], kv[:, :, 1]
    s = jnp.einsum('bthd,bmhd->bhtm', q, k).astype(jnp.float32)
    p = jax.nn.softmax(s, axis=-1).astype(v.dtype)
    return jnp.einsum('bhtm,bmhd->bthd', p, v).reshape(b, t, MEM_WIDTH)


def neighbourhood_attention(q, k, v, rpb):
    b, t, h, dh = q.shape
    rows = t // GRID_W
    kh = min(NA_KH, rows)
    qg = q.reshape(b, rows, GRID_W, h, dh) * (dh ** -0.5)
    kg = k.reshape(b, rows, GRID_W, h, dh)
    vg = v.reshape(b, rows, GRID_W, h, dh)
    col = jnp.arange(GRID_W)
    col_start = jnp.clip(col - NA_KW // 2, 0, GRID_W - NA_KW)
    col_in = (col[None, :] >= col_start[:, None]) & (col[None, :] < col_start[:, None] + NA_KW)
    dc = jnp.clip(col[None, :] - col[:, None], -(NA_KW - 1), NA_KW - 1) + NA_KW - 1

    def one_row(r):
        rs = jnp.clip(r - kh // 2, 0, rows - kh)
        qr = lax.dynamic_index_in_dim(qg, r, axis=1, keepdims=False)
        kb = lax.dynamic_slice_in_dim(kg, rs, kh, axis=1)
        vb = lax.dynamic_slice_in_dim(vg, rs, kh, axis=1)
        s = jnp.einsum('bqhd,bjkhd->bhqjk', qr, kb).astype(jnp.float32)
        dr = rs + jnp.arange(kh) - r + NA_KH - 1
        bias = rpb[:, dr[None, :, None], dc[:, None, :]].astype(jnp.float32)
        s = jnp.where(col_in[:, None, :], s + bias[None], -jnp.inf)
        p = jax.nn.softmax(s.reshape(b, h, GRID_W, kh * GRID_W), axis=-1)
        p = p.reshape(b, h, GRID_W, kh, GRID_W).astype(vb.dtype)
        return jnp.einsum('bhqjk,bjkhd->bqhd', p, vb)

    out = lax.map(one_row, jnp.arange(rows))
    return out.transpose(1, 0, 2, 3, 4).reshape(b, t, h * dh)


def ssd_branch(h, mem_n, w_in, conv_w, conv_b, dt_bias_f, dt_bias_b, a_log_f, a_log_b,
               d_skip, norm_g, w_mem_kv):
    b, t, _ = h.shape
    proj = h @ w_in
    z, xbc, dt_f, dt_b, q_mem = jnp.split(
        proj, [MIX_WIDTH, MIX_WIDTH + SSD_CONV_DIM, MIX_WIDTH + SSD_CONV_DIM + SSD_HEADS,
               MIX_WIDTH + SSD_CONV_DIM + 2 * SSD_HEADS], axis=-1)
    xbc = jax.nn.silu(depthwise_conv_centred(xbc, conv_w, conv_b))
    xs, bm, cm = jnp.split(xbc, [BRANCH_WIDTH, BRANCH_WIDTH + SSD_GROUPS * SSD_STATE], axis=-1)
    xs = xs.reshape(b, t, SSD_HEADS, SSD_HEAD_DIM)
    bm = bm.reshape(b, t, SSD_GROUPS, SSD_STATE)
    cm = cm.reshape(b, t, SSD_GROUPS, SSD_STATE)
    y_fwd = ssd_direction(xs, bm, cm, dt_f, dt_bias_f, a_log_f)
    y_bwd = jnp.flip(ssd_direction(jnp.flip(xs, 1), jnp.flip(bm, 1), jnp.flip(cm, 1),
                                   jnp.flip(dt_b, 1), dt_bias_b, a_log_b), 1)
    y = (y_fwd + y_bwd + d_skip[:, None] * xs).reshape(b, t, BRANCH_WIDTH)
    y = rmsnorm(y * jax.nn.silu(z[..., :BRANCH_WIDTH]), norm_g)
    m = memory_attention(q_mem, mem_n, w_mem_kv) * jax.nn.silu(z[..., BRANCH_WIDTH:])
    return jnp.concatenate([y, m.astype(y.dtype)], axis=-1)


def na_branch(h, mem_n, w_in, rpb, w_mem_kv):
    b, t, _ = h.shape
    proj = h @ w_in
    z, q, k, v, q_mem = jnp.split(
        proj, [MIX_WIDTH, MIX_WIDTH + BRANCH_WIDTH, MIX_WIDTH + 2 * BRANCH_WIDTH,
               MIX_WIDTH + 3 * BRANCH_WIDTH], axis=-1)
    shp = (b, t, NA_HEADS, NA_HEAD_DIM)
    o = neighbourhood_attention(q.reshape(shp), k.reshape(shp), v.reshape(shp), rpb)
    m = memory_attention(q_mem, mem_n, w_mem_kv)
    return jnp.concatenate([o, m.astype(o.dtype)], axis=-1) * jax.nn.silu(z)


def encoder(x, mem, pre_g, post_g, mem_g, w_mem_kv, w_out, ssd_w_in, ssd_conv_w, ssd_conv_b,
            ssd_dt_bias_f, ssd_dt_bias_b, ssd_a_log_f, ssd_a_log_b, ssd_d, ssd_norm_g,
            na_w_in, na_rpb):
    for i in range(DEPTH):
        h = rmsnorm(x, pre_g[i])
        mem_n = rmsnorm(mem, mem_g[i])
        j = i // N_MIXERS
        if i % N_MIXERS == 0:
            mixed = ssd_branch(h, mem_n, ssd_w_in[j], ssd_conv_w[j], ssd_conv_b[j], ssd_dt_bias_f[j],
                               ssd_dt_bias_b[j], ssd_a_log_f[j], ssd_a_log_b[j], ssd_d[j],
                               ssd_norm_g[j], w_mem_kv[i])
        else:
            mixed = na_branch(h, mem_n, na_w_in[j], na_rpb[j], w_mem_kv[i])
        x = x + rmsnorm(mixed @ w_out[i], post_g[i]).astype(x.dtype)
    return x


def setup_inputs(seed: int = 0) -> dict:
    key = jax.random.key(seed)
    ks = jax.random.split(key, 20)
    f32 = jnp.float32

    def nrm(k, shape, scale):
        return jax.random.normal(k, shape, f32) * scale

    def dt_bias(k):
        dt = jnp.exp(jax.random.uniform(k, (N_SSD_LAYERS, SSD_HEADS), f32,
                                        np.log(1e-3), np.log(1e-1)))
        return dt + jnp.log(-jnp.expm1(-dt))

    def a_log(k):
        return jnp.log(jax.random.uniform(k, (N_SSD_LAYERS, SSD_HEADS), f32, 1.0, 16.0))

    return {
        'x_prompt': nrm(ks[0], (BATCH, SEQ, D_MODEL), 1.0),
        'x_sample': nrm(ks[1], (DEC_BATCH, DEC_SEQ, D_MODEL), 1.0),
        'mem_prompt': nrm(ks[2], (BATCH, MEM_TOKENS, D_MODEL), 1.0),
        'mem_sample': nrm(ks[3], (DEC_BATCH, MEM_TOKENS, D_MODEL), 1.0),
        'pre_g': 1.0 + nrm(ks[4], (DEPTH, D_MODEL), 0.02),
        'post_g': 1.0 + nrm(ks[5], (DEPTH, D_MODEL), 0.02),
        'mem_g': 1.0 + nrm(ks[6], (DEPTH, D_MODEL), 0.02),
        'w_mem_kv': nrm(ks[7], (DEPTH, D_MODEL, 2 * MEM_WIDTH), D_MODEL ** -0.5),
        'w_out': nrm(ks[8], (DEPTH, MIX_WIDTH, D_MODEL), MIX_WIDTH ** -0.5),
        'ssd_w_in': nrm(ks[9], (N_SSD_LAYERS, D_MODEL, SSD_IN), D_MODEL ** -0.5),
        'ssd_conv_w': nrm(ks[10], (N_SSD_LAYERS, SSD_CONV, SSD_CONV_DIM), SSD_CONV ** -0.5),
        'ssd_conv_b': nrm(ks[11], (N_SSD_LAYERS, SSD_CONV_DIM), 0.02),
        'ssd_dt_bias_f': dt_bias(ks[12]),
        'ssd_dt_bias_b': dt_bias(ks[13]),
        'ssd_a_log_f': a_log(ks[14]),
        'ssd_a_log_b': a_log(ks[15]),
        'ssd_d': 1.0 + nrm(ks[16], (N_SSD_LAYERS, SSD_HEADS), 0.02),
        'ssd_norm_g': 1.0 + nrm(ks[17], (N_SSD_LAYERS, BRANCH_WIDTH), 0.02),
        'na_w_in': nrm(ks[18], (N_NA_LAYERS, D_MODEL, NA_IN), D_MODEL ** -0.5),
        'na_rpb': nrm(ks[19], (N_NA_LAYERS, NA_HEADS, 2 * NA_KH - 1, 2 * NA_KW - 1), 0.1),
    }


def reference(x_prompt, x_sample, mem_prompt, mem_sample, pre_g, post_g, mem_g, w_mem_kv, w_out,
              ssd_w_in, ssd_conv_w, ssd_conv_b, ssd_dt_bias_f, ssd_dt_bias_b, ssd_a_log_f,
              ssd_a_log_b, ssd_d, ssd_norm_g, na_w_in, na_rpb):
    y_prompt = encoder(x_prompt, mem_prompt, pre_g, post_g, mem_g, w_mem_kv, w_out, ssd_w_in,
                       ssd_conv_w, ssd_conv_b, ssd_dt_bias_f, ssd_dt_bias_b, ssd_a_log_f,
                       ssd_a_log_b, ssd_d, ssd_norm_g, na_w_in, na_rpb)
    y_sample = encoder(x_sample, mem_sample, pre_g, post_g, mem_g, w_mem_kv, w_out, ssd_w_in,
                       ssd_conv_w, ssd_conv_b, ssd_dt_bias_f, ssd_dt_bias_b, ssd_a_log_f,
                       ssd_a_log_b, ssd_d, ssd_norm_g, na_w_in, na_rpb)
    return (y_prompt, y_sample)
```

```python
import functools

import jax
import jax.numpy as jnp
from jax import lax
from jax.experimental import pallas as pl
from jax.experimental.pallas import tpu as pltpu

F32 = jnp.float32
BF16 = jnp.bfloat16

DEPTH = 4
D_MODEL = 1024
GRID_W = 64
BRANCH = 2048
HEAD_DIM = 64
SSD_HEADS = 32
SSD_GROUPS = 4
GROUP_HEADS = SSD_HEADS // SSD_GROUPS
GROUP_WIDTH = GROUP_HEADS * HEAD_DIM
SSD_STATE = 128
SSD_CONV = 5
CHUNK = 128
NA_HEADS = 32
NA_KH = 8
NA_KW = 16
MEM_TOKENS = 256
MEM_HEADS = 4
MEM_HEAD_DIM = 256
MEM_WIDTH = MEM_HEADS * MEM_HEAD_DIM
MIX_WIDTH = BRANCH + MEM_WIDTH
EPS = 1e-6

LANES = 128
CONV_TILE = 256
CONV_HALO = 8
VMEM_LIMIT = 56 * 1024 * 1024


def _params(*semantics):
    return pltpu.CompilerParams(dimension_semantics=semantics, vmem_limit_bytes=VMEM_LIMIT)


def _silu(v):
    return v * jax.nn.sigmoid(v)


def _softplus(v):
    return jnp.maximum(v, 0.0) + jnp.log1p(jnp.exp(-jnp.abs(v)))


def _split_bf16(v):
    hi = v.astype(BF16)
    lo = (v - hi.astype(F32)).astype(BF16)
    return jnp.concatenate([hi, lo], axis=1)


def _norm_matmul_body(x_ref, g_ref, w_ref, *rest, has_aux):
    if has_aux:
        waux_ref, o_ref, aux_ref, h_ref = rest
    else:
        o_ref, h_ref = rest

    @pl.when(pl.program_id(1) == 0)
    def _():
        x = x_ref[...]
        ms = jnp.mean(x * x, axis=-1, keepdims=True)
        h = (x * lax.rsqrt(ms + EPS) * g_ref[...]).astype(BF16)
        h_ref[...] = h
        if has_aux:
            aux_ref[...] = jnp.dot(h, waux_ref[...], preferred_element_type=F32)

    o_ref[...] = jnp.dot(h_ref[...], w_ref[...], preferred_element_type=F32).astype(o_ref.dtype)


def _norm_matmul(x, g, w, w_aux=None, *, name):
    m, d = x.shape
    wn = w.shape[1]
    tm = min(m, 1024)
    tn = min(wn, 1024)
    in_specs = [
        pl.BlockSpec((tm, d), lambda i, j: (i, 0)),
        pl.BlockSpec((1, d), lambda i, j: (0, 0)),
        pl.BlockSpec((d, tn), lambda i, j: (0, j)),
    ]
    out_shape = [jax.ShapeDtypeStruct((m, wn), BF16)]
    out_specs = [pl.BlockSpec((tm, tn), lambda i, j: (i, j))]
    args = [x, g, w]
    if w_aux is not None:
        an = w_aux.shape[1]
        in_specs.append(pl.BlockSpec((d, an), lambda i, j: (0, 0)))
        out_shape.append(jax.ShapeDtypeStruct((m, an), F32))
        out_specs.append(pl.BlockSpec((tm, an), lambda i, j: (i, 0)))
        args.append(w_aux)
    outs = pl.pallas_call(
        functools.partial(_norm_matmul_body, has_aux=w_aux is not None),
        grid=(m // tm, wn // tn),
        in_specs=in_specs,
        out_specs=out_specs,
        out_shape=out_shape,
        scratch_shapes=[pltpu.VMEM((tm, d), BF16)],
        compiler_params=_params("parallel", "arbitrary"),
        name=name,
    )(*args)
    return outs if w_aux is not None else outs[0]


def _conv_silu(src_ref, w_ref, b_ref, dst_ref, pad_ref, *, seq, width):
    n_tiles = seq // CONV_TILE
    zeros = jnp.zeros((CONV_HALO, LANES), F32)
    for cc in range(width // LANES):
        cols = slice(cc * LANES, (cc + 1) * LANES)
        pad_ref[0:CONV_HALO, :] = zeros
        pad_ref[seq + CONV_HALO:seq + 2 * CONV_HALO, :] = zeros

        def fill(i, carry, cols=cols):
            r0 = pl.multiple_of(i * CONV_TILE, CONV_TILE)
            dst = pl.multiple_of(r0 + CONV_HALO, CONV_HALO)
            pad_ref[pl.ds(dst, CONV_TILE), :] = src_ref[pl.ds(r0, CONV_TILE), cols].astype(F32)
            return carry

        lax.fori_loop(0, n_tiles, fill, 0)
        w = w_ref[:, cols]
        bias = b_ref[:, cols]

        def tile(i, carry, cols=cols, w=w, bias=bias):
            r0 = pl.multiple_of(i * CONV_TILE, CONV_TILE)
            win = pad_ref[pl.ds(r0, CONV_TILE + 2 * CONV_HALO), :]
            acc = jnp.broadcast_to(bias, (CONV_TILE, LANES))
            for k in range(SSD_CONV):
                off = CONV_HALO - SSD_CONV // 2 + k
                acc = acc + w[k:k + 1, :] * win[off:off + CONV_TILE, :]
            dst_ref[pl.ds(r0, CONV_TILE), cols] = _silu(acc).astype(BF16)
            return carry

        lax.fori_loop(0, n_tiles, tile, 0)


def _ssd_body(xs_ref, bp_ref, cp_ref, dt_ref, wx_ref, wb_ref, wc_ref, bx_ref, bb_ref, bc_ref,
              dtb_ref, alog_ref, dsk_ref, e_ref, y_ref,
              pad_ref, xc_ref, bcv_ref, ccv_ref, yf_ref, st_ref, *, seq):
    n_chunks = seq // CHUNK
    _conv_silu(xs_ref, wx_ref, bx_ref, xc_ref, pad_ref, seq=seq, width=GROUP_WIDTH)
    _conv_silu(bp_ref, wb_ref, bb_ref, bcv_ref, pad_ref, seq=seq, width=SSD_STATE)
    _conv_silu(cp_ref, wc_ref, bc_ref, ccv_ref, pad_ref, seq=seq, width=SSD_STATE)

    row = lax.broadcasted_iota(jnp.int32, (CHUNK, CHUNK), 0)
    col = lax.broadcasted_iota(jnp.int32, (CHUNK, CHUNK), 1)
    lane = lax.broadcasted_iota(jnp.int32, (CHUNK, LANES), 1)
    neg_a = -jnp.exp(alog_ref[...])

    def chunk(c, direction):
        r0 = pl.multiple_of(c * CHUNK, CHUNK)
        x = xc_ref[pl.ds(r0, CHUNK), :]
        bm = bcv_ref[pl.ds(r0, CHUNK), :]
        cm = ccv_ref[pl.ds(r0, CHUNK), :]
        dt = _softplus(dt_ref[pl.ds(r0, CHUNK), :] + dtb_ref[...])
        a = neg_a * dt
        keep = (col <= row) if direction == 0 else (col >= row)
        tri = jnp.where(keep, 1.0, 0.0).astype(BF16)
        acs2 = jnp.dot(tri, _split_bf16(a), preferred_element_type=F32)
        acs = acs2[:, :LANES] + acs2[:, LANES:]
        acs_t = acs.T
        edge = CHUNK - 1 if direction == 0 else 0
        to_end = jnp.exp(acs[edge:edge + 1, :] - acs)
        from_start = jnp.exp(acs)
        scal = jnp.concatenate([dt, to_end, from_start], axis=0)
        scal_x = jnp.dot(_split_bf16(scal), e_ref[direction], preferred_element_type=F32)
        dt_x = scal_x[0:CHUNK]
        to_end_x = scal_x[CHUNK:2 * CHUNK]
        from_start_x = scal_x[2 * CHUNK:3 * CHUNK]

        xf = x.astype(F32)
        xdt = xf * dt_x
        xdt_b = xdt.astype(BF16)
        cb = lax.dot_general(cm, bm, (((1,), (1,)), ((), ())), preferred_element_type=F32)
        y_parts = []
        for pair in range(GROUP_HEADS // 2):
            w_pair = []
            for hh in range(2):
                j = GROUP_HEADS * direction + 2 * pair + hh
                seg = acs[:, j:j + 1] - acs_t[j:j + 1, :]
                decay = jnp.exp(jnp.where(keep, seg, -jnp.inf))
                w_pair.append((decay * cb).astype(BF16))
            w_cat = jnp.concatenate(w_pair, axis=1)
            xp = xdt_b[:, pair * LANES:(pair + 1) * LANES]
            zero = jnp.zeros_like(xp)
            x_bd = jnp.concatenate([jnp.where(lane < HEAD_DIM, xp, zero),
                                    jnp.where(lane >= HEAD_DIM, xp, zero)], axis=0)
            y_parts.append(jnp.dot(w_cat, x_bd, preferred_element_type=F32))
        y_diag = jnp.concatenate(y_parts, axis=1)

        s_in = st_ref[...]
        y_off = jnp.dot(cm, s_in.astype(BF16), preferred_element_type=F32) * from_start_x
        xw = (xdt * to_end_x).astype(BF16)
        s_new = lax.dot_general(bm, xw, (((0,), (0,)), ((), ())), preferred_element_type=F32)
        st_ref[...] = s_in * from_start_x[edge:edge + 1, :] + s_new
        return r0, y_diag + y_off, xf

    st_ref[...] = jnp.zeros_like(st_ref)

    def fwd(c, carry):
        r0, y, _ = chunk(c, 0)
        yf_ref[pl.ds(r0, CHUNK), :] = y
        return carry

    lax.fori_loop(0, n_chunks, fwd, 0)
    st_ref[...] = jnp.zeros_like(st_ref)

    def bwd(i, carry):
        r0, y, xf = chunk(n_chunks - 1 - i, 1)
        y_ref[pl.ds(r0, CHUNK), :] = (yf_ref[pl.ds(r0, CHUNK), :] + y
                                      + dsk_ref[...] * xf).astype(y_ref.dtype)
        return carry

    lax.fori_loop(0, n_chunks, bwd, 0)


def _ssd_mixer(proj, dt_raw, p, *, batch, seq):
    rows = batch * seq
    gw, ns = GROUP_WIDTH, SSD_STATE
    xs0 = MIX_WIDTH // gw
    b0 = (MIX_WIDTH + BRANCH) // ns
    c0 = b0 + SSD_GROUPS
    in_specs = [
        pl.BlockSpec((seq, gw), lambda b, g: (b, xs0 + g)),
        pl.BlockSpec((seq, ns), lambda b, g: (b, b0 + g)),
        pl.BlockSpec((seq, ns), lambda b, g: (b, c0 + g)),
        pl.BlockSpec((seq, LANES), lambda b, g: (b, g)),
        pl.BlockSpec((SSD_CONV, gw), lambda b, g: (0, g)),
        pl.BlockSpec((SSD_CONV, ns), lambda b, g: (0, BRANCH // ns + g)),
        pl.BlockSpec((SSD_CONV, ns), lambda b, g: (0, BRANCH // ns + SSD_GROUPS + g)),
        pl.BlockSpec((1, gw), lambda b, g: (0, g)),
        pl.BlockSpec((1, ns), lambda b, g: (0, BRANCH // ns + g)),
        pl.BlockSpec((1, ns), lambda b, g: (0, BRANCH // ns + SSD_GROUPS + g)),
        pl.BlockSpec((1, LANES), lambda b, g: (0, g)),
        pl.BlockSpec((1, LANES), lambda b, g: (0, g)),
        pl.BlockSpec((1, gw), lambda b, g: (0, g)),
        pl.BlockSpec((2, 2 * LANES, gw), lambda b, g: (0, 0, 0)),
    ]
    return pl.pallas_call(
        functools.partial(_ssd_body, seq=seq),
        grid=(batch, SSD_GROUPS),
        in_specs=in_specs,
        out_specs=pl.BlockSpec((seq, gw), lambda b, g: (b, g)),
        out_shape=jax.ShapeDtypeStruct((rows, BRANCH), BF16),
        scratch_shapes=[
            pltpu.VMEM((seq + 2 * CONV_HALO, LANES), F32),
            pltpu.VMEM((seq, gw), BF16),
            pltpu.VMEM((seq, ns), BF16),
            pltpu.VMEM((seq, ns), BF16),
            pltpu.VMEM((seq, gw), F32),
            pltpu.VMEM((ns, gw), F32),
        ],
        compiler_params=_params("parallel", "parallel"),
        name="ssd_mixer",
    )(proj, proj, proj, dt_raw, p["conv_w"], p["conv_w"], p["conv_w"],
      p["conv_b"], p["conv_b"], p["conv_b"], p["dt_bias"], p["a_log"], p["d_skip"], p["expand"])


def _na_body(q_ref, k_ref, v_ref, bias_ref, o_ref, *, seq):
    n_rows = seq // GRID_W
    kh = min(NA_KH, n_rows)
    win = kh * GRID_W
    lane = lax.broadcasted_iota(jnp.int32, (GRID_W, LANES), 1)
    scale = HEAD_DIM ** -0.5

    def row_step(r, carry):
        rs = jnp.clip(r - kh // 2, 0, n_rows - kh)
        q0 = pl.multiple_of(r * GRID_W, GRID_W)
        k0 = pl.multiple_of(rs * GRID_W, GRID_W)
        q = q_ref[pl.ds(q0, GRID_W), :]
        kw = k_ref[pl.ds(k0, win), :]
        vw = v_ref[pl.ds(k0, win), :]
        outs = []
        for hh in range(2):
            mine = (lane < HEAD_DIM) if hh == 0 else (lane >= HEAD_DIM)
            qh = jnp.where(mine, q, jnp.zeros_like(q))
            s = lax.dot_general(qh, kw, (((1,), (1,)), ((), ())), preferred_element_type=F32)
            s = s * scale + bias_ref[r - rs, hh]
            m = jnp.max(s, axis=-1, keepdims=True)
            p = jnp.exp(s - m)
            denom = jnp.sum(p, axis=-1, keepdims=True)
            pv = jnp.dot(p.astype(BF16), vw, preferred_element_type=F32)
            outs.append(pv / denom)
        o_ref[pl.ds(q0, GRID_W), :] = jnp.where(lane < HEAD_DIM, outs[0], outs[1]).astype(o_ref.dtype)
        return carry

    lax.fori_loop(0, n_rows, row_step, 0)


def _na_mixer(proj, bias_tab, *, batch, seq):
    rows = batch * seq
    q0 = MIX_WIDTH // LANES
    k0 = q0 + BRANCH // LANES
    v0 = k0 + BRANCH // LANES
    n_off, _, _, win = bias_tab.shape
    return pl.pallas_call(
        functools.partial(_na_body, seq=seq),
        grid=(NA_HEADS // 2, batch),
        in_specs=[
            pl.BlockSpec((seq, LANES), lambda hp, b: (b, q0 + hp)),
            pl.BlockSpec((seq, LANES), lambda hp, b: (b, k0 + hp)),
            pl.BlockSpec((seq, LANES), lambda hp, b: (b, v0 + hp)),
            pl.BlockSpec((n_off, 2, GRID_W, win), lambda hp, b: (0, hp, 0, 0)),
        ],
        out_specs=pl.BlockSpec((seq, LANES), lambda hp, b: (b, hp)),
        out_shape=jax.ShapeDtypeStruct((rows, BRANCH), BF16),
        compiler_params=_params("parallel", "parallel"),
        name="na_mixer",
    )(proj, proj, proj, bias_tab)


def _na_bias_table(rpb, n_rows):
    kh = min(NA_KH, n_rows)
    col = jnp.arange(GRID_W)
    col_start = jnp.clip(col - NA_KW // 2, 0, GRID_W - NA_KW)
    col_in = (col[None, :] >= col_start[:, None]) & (col[None, :] < col_start[:, None] + NA_KW)
    dc = jnp.clip(col[None, :] - col[:, None], -(NA_KW - 1), NA_KW - 1) + NA_KW - 1
    off = jnp.arange(kh)
    dr = jnp.arange(kh)[None, :] - off[:, None] + NA_KH - 1
    bias = rpb[:, dr[:, :, None, None], dc[None, None, :, :]]
    bias = jnp.where(col_in[None, None, None], bias, -jnp.inf)
    bias = bias.transpose(1, 0, 3, 2, 4)
    return bias.reshape(kh, NA_HEADS, GRID_W, kh * GRID_W).astype(F32)


def _mem_attn_body(q_ref, z_ref, kv_ref, o_ref):
    scale = MEM_HEAD_DIM ** -0.5
    for h in range(MEM_HEADS):
        cols = slice(h * MEM_HEAD_DIM, (h + 1) * MEM_HEAD_DIM)
        vcols = slice(MEM_WIDTH + h * MEM_HEAD_DIM, MEM_WIDTH + (h + 1) * MEM_HEAD_DIM)
        s = lax.dot_general(q_ref[:, cols], kv_ref[:, cols], (((1,), (1,)), ((), ())),
                            preferred_element_type=F32) * scale
        m = jnp.max(s, axis=-1, keepdims=True)
        p = jnp.exp(s - m)
        denom = jnp.sum(p, axis=-1, keepdims=True)
        o = jnp.dot(p.astype(BF16), kv_ref[:, vcols], preferred_element_type=F32) / denom
        o_ref[:, cols] = (o * _silu(z_ref[:, cols].astype(F32))).astype(o_ref.dtype)


def _mem_attn(proj, kv, *, batch, seq, q_col):
    rows = batch * seq
    tq = min(seq, 512)
    per_seq = seq // tq
    return pl.pallas_call(
        _mem_attn_body,
        grid=(batch, per_seq),
        in_specs=[
            pl.BlockSpec((tq, MEM_WIDTH), lambda b, i: (b * per_seq + i, q_col)),
            pl.BlockSpec((tq, MEM_WIDTH), lambda b, i: (b * per_seq + i, BRANCH // MEM_WIDTH)),
            pl.BlockSpec((MEM_TOKENS, 2 * MEM_WIDTH), lambda b, i: (b, 0)),
        ],
        out_specs=pl.BlockSpec((tq, MEM_WIDTH), lambda b, i: (b * per_seq + i, 0)),
        out_shape=jax.ShapeDtypeStruct((rows, MEM_WIDTH), BF16),
        compiler_params=_params("parallel", "parallel"),
        name="mem_attn",
    )(proj, proj, kv)


def _out_proj_body(y_ref, z_ref, m_ref, x_ref, ng_ref, pg_ref, w_ref, o_ref, *, gated_norm):
    yg = y_ref[...].astype(F32) * _silu(z_ref[...].astype(F32))
    if gated_norm:
        ms = jnp.mean(yg * yg, axis=-1, keepdims=True)
        yg = yg * lax.rsqrt(ms + EPS) * ng_ref[...]
    acc = jnp.dot(yg.astype(BF16), w_ref[0:BRANCH, :], preferred_element_type=F32)
    acc = acc + jnp.dot(m_ref[...], w_ref[BRANCH:MIX_WIDTH, :], preferred_element_type=F32)
    ms = jnp.mean(acc * acc, axis=-1, keepdims=True)
    o_ref[...] = x_ref[...] + acc * lax.rsqrt(ms + EPS) * pg_ref[...]


def _out_proj(y, proj, m, x, norm_g, post_g, w_out, *, gated_norm):
    rows, d = x.shape
    tm = min(rows, 512)
    return pl.pallas_call(
        functools.partial(_out_proj_body, gated_norm=gated_norm),
        grid=(rows // tm,),
        in_specs=[
            pl.BlockSpec((tm, BRANCH), lambda i: (i, 0)),
            pl.BlockSpec((tm, BRANCH), lambda i: (i, 0)),
            pl.BlockSpec((tm, MEM_WIDTH), lambda i: (i, 0)),
            pl.BlockSpec((tm, d), lambda i: (i, 0)),
            pl.BlockSpec((1, BRANCH), lambda i: (0, 0)),
            pl.BlockSpec((1, d), lambda i: (0, 0)),
            pl.BlockSpec((MIX_WIDTH, d), lambda i: (0, 0)),
        ],
        out_specs=pl.BlockSpec((tm, d), lambda i: (i, 0)),
        out_shape=jax.ShapeDtypeStruct((rows, d), F32),
        compiler_params=_params("parallel"),
        name="out_proj",
    )(y, proj, m, x, norm_g, post_g, w_out)


def _per_group_lanes(fwd, bwd):
    n = fwd.shape[0]
    f = fwd.reshape(n, SSD_GROUPS, GROUP_HEADS)
    b = bwd.reshape(n, SSD_GROUPS, GROUP_HEADS)
    pad = jnp.zeros((n, SSD_GROUPS, LANES - 2 * GROUP_HEADS), F32)
    return jnp.concatenate([f, b, pad], axis=-1).reshape(n, 1, SSD_GROUPS * LANES)


def _prepare(pre_g, post_g, mem_g, w_mem_kv, w_out, ssd_w_in, ssd_conv_w, ssd_conv_b,
             ssd_dt_bias_f, ssd_dt_bias_b, ssd_a_log_f, ssd_a_log_b, ssd_d, ssd_norm_g,
             na_w_in, na_rpb):
    xbc_end = MIX_WIDTH + BRANCH + 2 * SSD_GROUPS * SSD_STATE
    dt_end = xbc_end + 2 * SSD_HEADS
    n_ssd = ssd_w_in.shape[0]
    w_dt_f = ssd_w_in[:, :, xbc_end:xbc_end + SSD_HEADS].reshape(n_ssd, D_MODEL, SSD_GROUPS, GROUP_HEADS)
    w_dt_b = ssd_w_in[:, :, xbc_end + SSD_HEADS:dt_end].reshape(n_ssd, D_MODEL, SSD_GROUPS, GROUP_HEADS)
    w_dt_pad = jnp.zeros((n_ssd, D_MODEL, SSD_GROUPS, LANES - 2 * GROUP_HEADS), F32)
    w_dt = jnp.concatenate([w_dt_f, w_dt_b, w_dt_pad], axis=-1).reshape(n_ssd, D_MODEL, SSD_GROUPS * LANES)
    w_main = jnp.concatenate([ssd_w_in[:, :, :xbc_end], ssd_w_in[:, :, dt_end:]], axis=-1)
    src = lax.broadcasted_iota(jnp.int32, (2, 2 * LANES, GROUP_WIDTH), 1) % LANES
    head = lax.broadcasted_iota(jnp.int32, (2, 2 * LANES, GROUP_WIDTH), 2) // HEAD_DIM
    direction = lax.broadcasted_iota(jnp.int32, (2, 2 * LANES, GROUP_WIDTH), 0)
    expand = (src == GROUP_HEADS * direction + head).astype(BF16)
    return dict(
        pre_g=pre_g[:, None, :], post_g=post_g[:, None, :], mem_g=mem_g[:, None, :],
        w_mem_kv=w_mem_kv.astype(BF16), w_out=w_out.astype(BF16),
        ssd_w_main=w_main.astype(BF16), ssd_w_dt=w_dt.astype(BF16),
        ssd=dict(conv_w=ssd_conv_w, conv_b=ssd_conv_b[:, None, :],
                 dt_bias=_per_group_lanes(ssd_dt_bias_f, ssd_dt_bias_b),
                 a_log=_per_group_lanes(ssd_a_log_f, ssd_a_log_b),
                 d_skip=jnp.repeat(ssd_d, HEAD_DIM, axis=-1)[:, None, :], expand=expand),
        ssd_norm_g=ssd_norm_g[:, None, :],
        na_w_in=na_w_in.astype(BF16), na_rpb=na_rpb,
    )


def _encoder(x, mem, p):
    batch, seq, d = x.shape
    xf = x.reshape(batch * seq, d)
    memf = mem.reshape(batch * MEM_TOKENS, d)
    bias_tabs = [_na_bias_table(p["na_rpb"][j], seq // GRID_W) for j in range(p["na_rpb"].shape[0])]
    for i in range(DEPTH):
        j = i // 2
        kv = _norm_matmul(memf, p["mem_g"][i], p["w_mem_kv"][i], name="mem_kv")
        if i % 2 == 0:
            proj, dt_raw = _norm_matmul(xf, p["pre_g"][i], p["ssd_w_main"][j], p["ssd_w_dt"][j],
                                        name="ssd_in_proj")
            layer = {k: v[j] for k, v in p["ssd"].items() if k != "expand"}
            layer["expand"] = p["ssd"]["expand"]
            y = _ssd_mixer(proj, dt_raw, layer, batch=batch, seq=seq)
            q_col = (MIX_WIDTH + BRANCH + 2 * SSD_GROUPS * SSD_STATE) // MEM_WIDTH
            norm_g = p["ssd_norm_g"][j]
        else:
            proj = _norm_matmul(xf, p["pre_g"][i], p["na_w_in"][j], name="na_in_proj")
            y = _na_mixer(proj, bias_tabs[j], batch=batch, seq=seq)
            q_col = (MIX_WIDTH + 3 * BRANCH) // MEM_WIDTH
            norm_g = p["ssd_norm_g"][0]
        m = _mem_attn(proj, kv, batch=batch, seq=seq, q_col=q_col)
        xf = _out_proj(y, proj, m, xf, norm_g, p["post_g"][i], p["w_out"][i], gated_norm=(i % 2 == 0))
    return xf.reshape(batch, seq, d)


def kernel(x_prompt, x_sample, mem_prompt, mem_sample, pre_g, post_g, mem_g, w_mem_kv, w_out, ssd_w_in, ssd_conv_w, ssd_conv_b, ssd_dt_bias_f, ssd_dt_bias_b, ssd_a_log_f, ssd_a_log_b, ssd_d, ssd_norm_g, na_w_in, na_rpb):
    p = _prepare(pre_g, post_g, mem_g, w_mem_kv, w_out, ssd_w_in, ssd_conv_w, ssd_conv_b,
                 ssd_dt_bias_f, ssd_dt_bias_b, ssd_a_log_f, ssd_a_log_b, ssd_d, ssd_norm_g,
                 na_w_in, na_rpb)
    return _encoder(x_prompt, mem_prompt, p), _encoder(x_sample, mem_sample, p)
```

```python
import functools

import jax
import jax.numpy as jnp
from jax import lax
from jax.experimental import pallas as pl
from jax.experimental.pallas import tpu as pltpu

F32 = jnp.float32
BF16 = jnp.bfloat16

DEPTH = 4
D_MODEL = 1024
GRID_W = 64
BRANCH = 2048
HEAD_DIM = 64
SSD_HEADS = 32
SSD_GROUPS = 4
GROUP_HEADS = SSD_HEADS // SSD_GROUPS
GROUP_WIDTH = GROUP_HEADS * HEAD_DIM
SSD_STATE = 128
SSD_CONV = 5
CHUNK = 128
NA_HEADS = 32
NA_KH = 8
NA_KW = 16
MEM_TOKENS = 256
MEM_HEADS = 4
MEM_HEAD_DIM = 256
MEM_WIDTH = MEM_HEADS * MEM_HEAD_DIM
MIX_WIDTH = BRANCH + MEM_WIDTH
EPS = 1e-6

LANES = 128
SSD_CHUNKS_PER_STEP = 2
NA_ROWS_PER_STEP = 8
CONV_TILE = 256
CONV_HALO = 8
VMEM_LIMIT = 56 * 1024 * 1024


def _params(*semantics):
    return pltpu.CompilerParams(dimension_semantics=semantics, vmem_limit_bytes=VMEM_LIMIT)


def _silu(v):
    return v * jax.nn.sigmoid(v)


def _softplus(v):
    return jnp.maximum(v, 0.0) + jnp.log1p(jnp.exp(-jnp.abs(v)))


def _split_bf16(v):
    hi = v.astype(BF16)
    lo = (v - hi.astype(F32)).astype(BF16)
    return jnp.concatenate([hi, lo], axis=1)


def _norm_matmul_body(x_ref, g_ref, w_ref, *rest, has_aux):
    if has_aux:
        waux_ref, o_ref, aux_ref, h_ref = rest
    else:
        o_ref, h_ref = rest

    @pl.when(pl.program_id(1) == 0)
    def _():
        x = x_ref[...]
        ms = jnp.mean(x * x, axis=-1, keepdims=True)
        h = (x * lax.rsqrt(ms + EPS) * g_ref[...]).astype(BF16)
        h_ref[...] = h
        if has_aux:
            aux_ref[...] = jnp.dot(h, waux_ref[...], preferred_element_type=F32)

    o_ref[...] = jnp.dot(h_ref[...], w_ref[...], preferred_element_type=F32).astype(o_ref.dtype)


def _norm_matmul(x, g, w, w_aux=None, *, name):
    m, d = x.shape
    wn = w.shape[1]
    tm = min(m, 1024)
    tn = min(wn, 1024)
    in_specs = [
        pl.BlockSpec((tm, d), lambda i, j: (i, 0)),
        pl.BlockSpec((1, d), lambda i, j: (0, 0)),
        pl.BlockSpec((d, tn), lambda i, j: (0, j)),
    ]
    out_shape = [jax.ShapeDtypeStruct((m, wn), BF16)]
    out_specs = [pl.BlockSpec((tm, tn), lambda i, j: (i, j))]
    args = [x, g, w]
    if w_aux is not None:
        an = w_aux.shape[1]
        in_specs.append(pl.BlockSpec((d, an), lambda i, j: (0, 0)))
        out_shape.append(jax.ShapeDtypeStruct((m, an), F32))
        out_specs.append(pl.BlockSpec((tm, an), lambda i, j: (i, 0)))
        args.append(w_aux)
    outs = pl.pallas_call(
        functools.partial(_norm_matmul_body, has_aux=w_aux is not None),
        grid=(m // tm, wn // tn),
        in_specs=in_specs,
        out_specs=out_specs,
        out_shape=out_shape,
        scratch_shapes=[pltpu.VMEM((tm, d), BF16)],
        compiler_params=_params("parallel", "arbitrary"),
        name=name,
    )(*args)
    return outs if w_aux is not None else outs[0]


def _conv_silu(src_ref, w_ref, b_ref, dst_ref, pad_ref, *, seq, width):
    n_tiles = seq // CONV_TILE
    zeros = jnp.zeros((CONV_HALO, LANES), F32)
    for cc in range(width // LANES):
        cols = slice(cc * LANES, (cc + 1) * LANES)
        pad_ref[0:CONV_HALO, :] = zeros
        pad_ref[seq + CONV_HALO:seq + 2 * CONV_HALO, :] = zeros

        def fill(i, carry, cols=cols):
            r0 = pl.multiple_of(i * CONV_TILE, CONV_TILE)
            dst = pl.multiple_of(r0 + CONV_HALO, CONV_HALO)
            pad_ref[pl.ds(dst, CONV_TILE), :] = src_ref[pl.ds(r0, CONV_TILE), cols].astype(F32)
            return carry

        lax.fori_loop(0, n_tiles, fill, 0)
        w = w_ref[:, cols]
        bias = b_ref[:, cols]

        def tile(i, carry, cols=cols, w=w, bias=bias):
            r0 = pl.multiple_of(i * CONV_TILE, CONV_TILE)
            win = pad_ref[pl.ds(r0, CONV_TILE + 2 * CONV_HALO), :]
            acc = jnp.broadcast_to(bias, (CONV_TILE, LANES))
            for k in range(SSD_CONV):
                off = CONV_HALO - SSD_CONV // 2 + k
                acc = acc + w[k:k + 1, :] * win[off:off + CONV_TILE, :]
            dst_ref[pl.ds(r0, CONV_TILE), cols] = _silu(acc).astype(BF16)
            return carry

        lax.fori_loop(0, n_tiles, tile, 0)


def _ssd_body(xs_ref, bp_ref, cp_ref, dt_ref, wx_ref, wb_ref, wc_ref, bx_ref, bb_ref, bc_ref,
              dtb_ref, alog_ref, dsk_ref, e_ref, y_ref,
              pad_ref, xc_ref, bcv_ref, ccv_ref, part_ref, stf_ref, stb_ref, *, seq):
    n_chunks = seq // CHUNK
    assert n_chunks % (2 * SSD_CHUNKS_PER_STEP) == 0
    _conv_silu(xs_ref, wx_ref, bx_ref, xc_ref, pad_ref, seq=seq, width=GROUP_WIDTH)
    _conv_silu(bp_ref, wb_ref, bb_ref, bcv_ref, pad_ref, seq=seq, width=SSD_STATE)
    _conv_silu(cp_ref, wc_ref, bc_ref, ccv_ref, pad_ref, seq=seq, width=SSD_STATE)

    row = lax.broadcasted_iota(jnp.int32, (CHUNK, CHUNK), 0)
    col = lax.broadcasted_iota(jnp.int32, (CHUNK, CHUNK), 1)
    lane = lax.broadcasted_iota(jnp.int32, (CHUNK, LANES), 1)
    neg_a = -jnp.exp(alog_ref[...])

    def prepare(c, direction):
        r0 = pl.multiple_of(c * CHUNK, CHUNK)
        dt = _softplus(dt_ref[pl.ds(r0, CHUNK), :] + dtb_ref[...])
        a = neg_a * dt
        keep = (col <= row) if direction == 0 else (col >= row)
        tri = jnp.where(keep, 1.0, 0.0).astype(BF16)
        acs2 = jnp.dot(tri, _split_bf16(a), preferred_element_type=F32)
        acs = acs2[:, :LANES] + acs2[:, LANES:]
        edge = CHUNK - 1 if direction == 0 else 0
        to_end_dt = jnp.exp(acs[edge:edge + 1, :] - acs) * dt
        from_start = jnp.exp(acs)
        scal = jnp.concatenate([to_end_dt, from_start], axis=0)
        scal_x = jnp.dot(_split_bf16(scal), e_ref[direction], preferred_element_type=F32)
        return dict(r0=r0, direction=direction, keep=keep, edge=edge, acs=acs, acs_t=acs.T, dt_t=dt.T,
                    to_end_dt_x=scal_x[0:CHUNK], from_start_x=scal_x[CHUNK:2 * CHUNK])

    def within_chunk(pr):
        r0, direction = pr["r0"], pr["direction"]
        x = xc_ref[pl.ds(r0, CHUNK), :]
        cb = lax.dot_general(ccv_ref[pl.ds(r0, CHUNK), :], bcv_ref[pl.ds(r0, CHUNK), :],
                             (((1,), (1,)), ((), ())), preferred_element_type=F32)
        y_parts = []
        for pair in range(GROUP_HEADS // 2):
            w_pair = []
            for hh in range(2):
                j = GROUP_HEADS * direction + 2 * pair + hh
                seg = pr["acs"][:, j:j + 1] - pr["acs_t"][j:j + 1, :]
                decay = jnp.exp(jnp.where(pr["keep"], seg, -jnp.inf))
                w_pair.append((decay * (cb * pr["dt_t"][j:j + 1, :])).astype(BF16))
            w_cat = jnp.concatenate(w_pair, axis=1)
            xp = x[:, pair * LANES:(pair + 1) * LANES]
            zero = jnp.zeros_like(xp)
            x_bd = jnp.concatenate([jnp.where(lane < HEAD_DIM, xp, zero),
                                    jnp.where(lane >= HEAD_DIM, xp, zero)], axis=0)
            y_parts.append(jnp.dot(w_cat, x_bd, preferred_element_type=F32))
        return jnp.concatenate(y_parts, axis=1)

    def across_chunks(pr, st_ref):
        r0, edge = pr["r0"], pr["edge"]
        s_in = st_ref[...]
        y_off = jnp.dot(ccv_ref[pl.ds(r0, CHUNK), :], s_in.astype(BF16),
                        preferred_element_type=F32) * pr["from_start_x"]
        xw = (xc_ref[pl.ds(r0, CHUNK), :].astype(F32) * pr["to_end_dt_x"]).astype(BF16)
        s_new = lax.dot_general(bcv_ref[pl.ds(r0, CHUNK), :], xw, (((0,), (0,)), ((), ())),
                                preferred_element_type=F32)
        st_ref[...] = s_in * pr["from_start_x"][edge:edge + 1, :] + s_new
        return y_off

    stf_ref[...] = jnp.zeros_like(stf_ref)
    stb_ref[...] = jnp.zeros_like(stb_ref)
    n_steps = n_chunks // SSD_CHUNKS_PER_STEP

    def step(i, combine):
        chains = []
        for u in range(SSD_CHUNKS_PER_STEP):
            c = i * SSD_CHUNKS_PER_STEP + u
            chains += [(c, 0, stf_ref), (n_chunks - 1 - c, 1, stb_ref)]
        prepared = [prepare(c, d) for c, d, _ in chains]
        y_diag = [within_chunk(pr) for pr in prepared]
        for (_, d, st_ref), pr, yd in zip(chains, prepared, y_diag):
            r0 = pr["r0"]
            y = yd + across_chunks(pr, st_ref)
            if d == 0:
                y = y + dsk_ref[...] * xc_ref[pl.ds(r0, CHUNK), :].astype(F32)
            if combine:
                y_ref[pl.ds(r0, CHUNK), :] = (part_ref[pl.ds(r0, CHUNK), :] + y).astype(y_ref.dtype)
            else:
                part_ref[pl.ds(r0, CHUNK), :] = y

    def first_half(i, carry):
        step(i, False)
        return carry

    def second_half(i, carry):
        step(i, True)
        return carry

    lax.fori_loop(0, n_steps // 2, first_half, 0)
    lax.fori_loop(n_steps // 2, n_steps, second_half, 0)


def _ssd_mixer(proj, dt_raw, p, *, batch, seq):
    rows = batch * seq
    gw, ns = GROUP_WIDTH, SSD_STATE
    xs0 = MIX_WIDTH // gw
    b0 = (MIX_WIDTH + BRANCH) // ns
    c0 = b0 + SSD_GROUPS
    in_specs = [
        pl.BlockSpec((seq, gw), lambda b, g: (b, xs0 + g)),
        pl.BlockSpec((seq, ns), lambda b, g: (b, b0 + g)),
        pl.BlockSpec((seq, ns), lambda b, g: (b, c0 + g)),
        pl.BlockSpec((seq, LANES), lambda b, g: (b, g)),
        pl.BlockSpec((SSD_CONV, gw), lambda b, g: (0, g)),
        pl.BlockSpec((SSD_CONV, ns), lambda b, g: (0, BRANCH // ns + g)),
        pl.BlockSpec((SSD_CONV, ns), lambda b, g: (0, BRANCH // ns + SSD_GROUPS + g)),
        pl.BlockSpec((1, gw), lambda b, g: (0, g)),
        pl.BlockSpec((1, ns), lambda b, g: (0, BRANCH // ns + g)),
        pl.BlockSpec((1, ns), lambda b, g: (0, BRANCH // ns + SSD_GROUPS + g)),
        pl.BlockSpec((1, LANES), lambda b, g: (0, g)),
        pl.BlockSpec((1, LANES), lambda b, g: (0, g)),
        pl.BlockSpec((1, gw), lambda b, g: (0, g)),
        pl.BlockSpec((2, 2 * LANES, gw), lambda b, g: (0, 0, 0)),
    ]
    return pl.pallas_call(
        functools.partial(_ssd_body, seq=seq),
        grid=(batch, SSD_GROUPS),
        in_specs=in_specs,
        out_specs=pl.BlockSpec((seq, gw), lambda b, g: (b, g)),
        out_shape=jax.ShapeDtypeStruct((rows, BRANCH), BF16),
        scratch_shapes=[
            pltpu.VMEM((seq + 2 * CONV_HALO, LANES), F32),
            pltpu.VMEM((seq, gw), BF16),
            pltpu.VMEM((seq, ns), BF16),
            pltpu.VMEM((seq, ns), BF16),
            pltpu.VMEM((seq, gw), F32),
            pltpu.VMEM((ns, gw), F32),
            pltpu.VMEM((ns, gw), F32),
        ],
        compiler_params=_params("parallel", "parallel"),
        name="ssd_mixer",
    )(proj, proj, proj, dt_raw, p["conv_w"], p["conv_w"], p["conv_w"],
      p["conv_b"], p["conv_b"], p["conv_b"], p["dt_bias"], p["a_log"], p["d_skip"], p["expand"])


def _na_body(q_ref, k_ref, v_ref, bias_ref, o_ref, *, seq):
    n_rows = seq // GRID_W
    kh = min(NA_KH, n_rows)
    win = kh * GRID_W
    assert n_rows % NA_ROWS_PER_STEP == 0
    lane = lax.broadcasted_iota(jnp.int32, (GRID_W, LANES), 1)
    lane2 = lax.broadcasted_iota(jnp.int32, (2 * GRID_W, LANES), 1)
    row2 = lax.broadcasted_iota(jnp.int32, (2 * GRID_W, LANES), 0)
    own_channels = (lane2 >= HEAD_DIM) == (row2 >= GRID_W)
    scale = HEAD_DIM ** -0.5

    def scores(r):
        rs = jnp.clip(r - kh // 2, 0, n_rows - kh)
        q0 = pl.multiple_of(r * GRID_W, GRID_W)
        k0 = pl.multiple_of(rs * GRID_W, GRID_W)
        q = q_ref[pl.ds(q0, GRID_W), :]
        qq = jnp.concatenate([q, q], axis=0)
        qq = jnp.where(own_channels, qq, jnp.zeros_like(qq))
        s = lax.dot_general(qq, k_ref[pl.ds(k0, win), :], (((1,), (1,)), ((), ())),
                            preferred_element_type=F32)
        return q0, k0, r - rs, s

    def softmax(off, s):
        s = s * scale + bias_ref[off].reshape(2 * GRID_W, win)
        p = jnp.exp(s - jnp.max(s, axis=-1, keepdims=True))
        return p.astype(BF16), jnp.sum(p, axis=-1, keepdims=True)

    def weighted_values(q0, k0, p, denom):
        pv = jnp.dot(p, v_ref[pl.ds(k0, win), :], preferred_element_type=F32) / denom
        o = jnp.where(lane < HEAD_DIM, pv[0:GRID_W], pv[GRID_W:2 * GRID_W])
        o_ref[pl.ds(q0, GRID_W), :] = o.astype(o_ref.dtype)

    def rows_step(i, carry):
        rows = [scores(i * NA_ROWS_PER_STEP + u) for u in range(NA_ROWS_PER_STEP)]
        probs = [softmax(off, s) for _, _, off, s in rows]
        for (q0, k0, _, _), (p, denom) in zip(rows, probs):
            weighted_values(q0, k0, p, denom)
        return carry

    lax.fori_loop(0, n_rows // NA_ROWS_PER_STEP, rows_step, 0)


def _na_mixer(proj, bias_tab, *, batch, seq):
    rows = batch * seq
    q0 = MIX_WIDTH // LANES
    k0 = q0 + BRANCH // LANES
    v0 = k0 + BRANCH // LANES
    n_off, _, _, win = bias_tab.shape
    return pl.pallas_call(
        functools.partial(_na_body, seq=seq),
        grid=(NA_HEADS // 2, batch),
        in_specs=[
            pl.BlockSpec((seq, LANES), lambda hp, b: (b, q0 + hp)),
            pl.BlockSpec((seq, LANES), lambda hp, b: (b, k0 + hp)),
            pl.BlockSpec((seq, LANES), lambda hp, b: (b, v0 + hp)),
            pl.BlockSpec((n_off, 2, GRID_W, win), lambda hp, b: (0, hp, 0, 0)),
        ],
        out_specs=pl.BlockSpec((seq, LANES), lambda hp, b: (b, hp)),
        out_shape=jax.ShapeDtypeStruct((rows, BRANCH), BF16),
        compiler_params=_params("parallel", "parallel"),
        name="na_mixer",
    )(proj, proj, proj, bias_tab)


def _na_bias_table(rpb, n_rows):
    kh = min(NA_KH, n_rows)
    col = jnp.arange(GRID_W)
    col_start = jnp.clip(col - NA_KW // 2, 0, GRID_W - NA_KW)
    col_in = (col[None, :] >= col_start[:, None]) & (col[None, :] < col_start[:, None] + NA_KW)
    dc = jnp.clip(col[None, :] - col[:, None], -(NA_KW - 1), NA_KW - 1) + NA_KW - 1
    by_dc = jnp.take(rpb, dc.reshape(-1), axis=2).reshape(NA_HEADS, 2 * NA_KH - 1, GRID_W, GRID_W)
    by_dc = jnp.where(col_in[None, None], by_dc, -jnp.inf).transpose(0, 2, 1, 3)
    tabs = [by_dc[:, :, NA_KH - 1 - off:NA_KH - 1 - off + kh, :].reshape(NA_HEADS, GRID_W, kh * GRID_W)
            for off in range(kh)]
    return jnp.stack(tabs, axis=0).astype(F32)


def _mem_attn_body(q_ref, z_ref, kv_ref, o_ref):
    scale = MEM_HEAD_DIM ** -0.5
    for h in range(MEM_HEADS):
        cols = slice(h * MEM_HEAD_DIM, (h + 1) * MEM_HEAD_DIM)
        vcols = slice(MEM_WIDTH + h * MEM_HEAD_DIM, MEM_WIDTH + (h + 1) * MEM_HEAD_DIM)
        s = lax.dot_general(q_ref[:, cols], kv_ref[:, cols], (((1,), (1,)), ((), ())),
                            preferred_element_type=F32) * scale
        m = jnp.max(s, axis=-1, keepdims=True)
        p = jnp.exp(s - m)
        denom = jnp.sum(p, axis=-1, keepdims=True)
        o = jnp.dot(p.astype(BF16), kv_ref[:, vcols], preferred_element_type=F32) / denom
        o_ref[:, cols] = (o * _silu(z_ref[:, cols].astype(F32))).astype(o_ref.dtype)


def _mem_attn(proj, kv, *, batch, seq, q_col):
    rows = batch * seq
    tq = min(seq, 512)
    per_seq = seq // tq
    return pl.pallas_call(
        _mem_attn_body,
        grid=(batch, per_seq),
        in_specs=[
            pl.BlockSpec((tq, MEM_WIDTH), lambda b, i: (b * per_seq + i, q_col)),
            pl.BlockSpec((tq, MEM_WIDTH), lambda b, i: (b * per_seq + i, BRANCH // MEM_WIDTH)),
            pl.BlockSpec((MEM_TOKENS, 2 * MEM_WIDTH), lambda b, i: (b, 0)),
        ],
        out_specs=pl.BlockSpec((tq, MEM_WIDTH), lambda b, i: (b * per_seq + i, 0)),
        out_shape=jax.ShapeDtypeStruct((rows, MEM_WIDTH), BF16),
        compiler_params=_params("parallel", "parallel"),
        name="mem_attn",
    )(proj, proj, kv)


def _out_proj_body(y_ref, z_ref, m_ref, x_ref, ng_ref, pg_ref, w_ref, o_ref, *, gated_norm):
    yg = y_ref[...].astype(F32) * _silu(z_ref[...].astype(F32))
    if gated_norm:
        ms = jnp.mean(yg * yg, axis=-1, keepdims=True)
        yg = yg * lax.rsqrt(ms + EPS) * ng_ref[...]
    acc = jnp.dot(yg.astype(BF16), w_ref[0:BRANCH, :], preferred_element_type=F32)
    acc = acc + jnp.dot(m_ref[...], w_ref[BRANCH:MIX_WIDTH, :], preferred_element_type=F32)
    ms = jnp.mean(acc * acc, axis=-1, keepdims=True)
    o_ref[...] = x_ref[...] + acc * lax.rsqrt(ms + EPS) * pg_ref[...]


def _out_proj(y, proj, m, x, norm_g, post_g, w_out, *, gated_norm):
    rows, d = x.shape
    tm = min(rows, 512)
    return pl.pallas_call(
        functools.partial(_out_proj_body, gated_norm=gated_norm),
        grid=(rows // tm,),
        in_specs=[
            pl.BlockSpec((tm, BRANCH), lambda i: (i, 0)),
            pl.BlockSpec((tm, BRANCH), lambda i: (i, 0)),
            pl.BlockSpec((tm, MEM_WIDTH), lambda i: (i, 0)),
            pl.BlockSpec((tm, d), lambda i: (i, 0)),
            pl.BlockSpec((1, BRANCH), lambda i: (0, 0)),
            pl.BlockSpec((1, d), lambda i: (0, 0)),
            pl.BlockSpec((MIX_WIDTH, d), lambda i: (0, 0)),
        ],
        out_specs=pl.BlockSpec((tm, d), lambda i: (i, 0)),
        out_shape=jax.ShapeDtypeStruct((rows, d), F32),
        compiler_params=_params("parallel"),
        name="out_proj",
    )(y, proj, m, x, norm_g, post_g, w_out)


def _per_group_lanes(fwd, bwd):
    n = fwd.shape[0]
    f = fwd.reshape(n, SSD_GROUPS, GROUP_HEADS)
    b = bwd.reshape(n, SSD_GROUPS, GROUP_HEADS)
    pad = jnp.zeros((n, SSD_GROUPS, LANES - 2 * GROUP_HEADS), F32)
    return jnp.concatenate([f, b, pad], axis=-1).reshape(n, 1, SSD_GROUPS * LANES)


def _prepare(pre_g, post_g, mem_g, w_mem_kv, w_out, ssd_w_in, ssd_conv_w, ssd_conv_b,
             ssd_dt_bias_f, ssd_dt_bias_b, ssd_a_log_f, ssd_a_log_b, ssd_d, ssd_norm_g,
             na_w_in, na_rpb):
    xbc_end = MIX_WIDTH + BRANCH + 2 * SSD_GROUPS * SSD_STATE
    dt_end = xbc_end + 2 * SSD_HEADS
    n_ssd = ssd_w_in.shape[0]
    w_dt_f = ssd_w_in[:, :, xbc_end:xbc_end + SSD_HEADS].reshape(n_ssd, D_MODEL, SSD_GROUPS, GROUP_HEADS)
    w_dt_b = ssd_w_in[:, :, xbc_end + SSD_HEADS:dt_end].reshape(n_ssd, D_MODEL, SSD_GROUPS, GROUP_HEADS)
    w_dt_pad = jnp.zeros((n_ssd, D_MODEL, SSD_GROUPS, LANES - 2 * GROUP_HEADS), F32)
    w_dt = jnp.concatenate([w_dt_f, w_dt_b, w_dt_pad], axis=-1).reshape(n_ssd, D_MODEL, SSD_GROUPS * LANES)
    w_main = jnp.concatenate([ssd_w_in[:, :, :xbc_end], ssd_w_in[:, :, dt_end:]], axis=-1)
    src = lax.broadcasted_iota(jnp.int32, (2, 2 * LANES, GROUP_WIDTH), 1) % LANES
    head = lax.broadcasted_iota(jnp.int32, (2, 2 * LANES, GROUP_WIDTH), 2) // HEAD_DIM
    direction = lax.broadcasted_iota(jnp.int32, (2, 2 * LANES, GROUP_WIDTH), 0)
    expand = (src == GROUP_HEADS * direction + head).astype(BF16)
    return dict(
        pre_g=pre_g[:, None, :], post_g=post_g[:, None, :], mem_g=mem_g[:, None, :],
        w_mem_kv=w_mem_kv.astype(BF16), w_out=w_out.astype(BF16),
        ssd_w_main=w_main.astype(BF16), ssd_w_dt=w_dt.astype(BF16),
        ssd=dict(conv_w=ssd_conv_w, conv_b=ssd_conv_b[:, None, :],
                 dt_bias=_per_group_lanes(ssd_dt_bias_f, ssd_dt_bias_b),
                 a_log=_per_group_lanes(ssd_a_log_f, ssd_a_log_b),
                 d_skip=jnp.repeat(ssd_d, HEAD_DIM, axis=-1)[:, None, :], expand=expand),
        ssd_norm_g=ssd_norm_g[:, None, :],
        na_w_in=na_w_in.astype(BF16), na_rpb=na_rpb,
    )


def _encoder(x, mem, p):
    batch, seq, d = x.shape
    xf = x.reshape(batch * seq, d)
    memf = mem.reshape(batch * MEM_TOKENS, d)
    bias_tabs = [_na_bias_table(p["na_rpb"][j], seq // GRID_W) for j in range(p["na_rpb"].shape[0])]
    for i in range(DEPTH):
        j = i // 2
        kv = _norm_matmul(memf, p["mem_g"][i], p["w_mem_kv"][i], name="mem_kv")
        if i % 2 == 0:
            proj, dt_raw = _norm_matmul(xf, p["pre_g"][i], p["ssd_w_main"][j], p["ssd_w_dt"][j],
                                        name="ssd_in_proj")
            layer = {k: v[j] for k, v in p["ssd"].items() if k != "expand"}
            layer["expand"] = p["ssd"]["expand"]
            y = _ssd_mixer(proj, dt_raw, layer, batch=batch, seq=seq)
            q_col = (MIX_WIDTH + BRANCH + 2 * SSD_GROUPS * SSD_STATE) // MEM_WIDTH
            norm_g = p["ssd_norm_g"][j]
        else:
            proj = _norm_matmul(xf, p["pre_g"][i], p["na_w_in"][j], name="na_in_proj")
            y = _na_mixer(proj, bias_tabs[j], batch=batch, seq=seq)
            q_col = (MIX_WIDTH + 3 * BRANCH) // MEM_WIDTH
            norm_g = p["ssd_norm_g"][0]
        m = _mem_attn(proj, kv, batch=batch, seq=seq, q_col=q_col)
        xf = _out_proj(y, proj, m, xf, norm_g, p["post_g"][i], p["w_out"][i], gated_norm=(i % 2 == 0))
    return xf.reshape(batch, seq, d)


def kernel(x_prompt, x_sample, mem_prompt, mem_sample, pre_g, post_g, mem_g, w_mem_kv, w_out, ssd_w_in, ssd_conv_w, ssd_conv_b, ssd_dt_bias_f, ssd_dt_bias_b, ssd_a_log_f, ssd_a_log_b, ssd_d, ssd_norm_g, na_w_in, na_rpb):
    p = _prepare(pre_g, post_g, mem_g, w_mem_kv, w_out, ssd_w_in, ssd_conv_w, ssd_conv_b,
                 ssd_dt_bias_f, ssd_dt_bias_b, ssd_a_log_f, ssd_a_log_b, ssd_d, ssd_norm_g,
                 na_w_in, na_rpb)
    return _encoder(x_prompt, mem_prompt, p), _encoder(x_sample, mem_sample, p)
```

```python
import functools

import jax
import jax.numpy as jnp
from jax import lax
from jax.experimental import pallas as pl
from jax.experimental.pallas import tpu as pltpu

F32 = jnp.float32
BF16 = jnp.bfloat16

DEPTH = 4
D_MODEL = 1024
GRID_W = 64
BRANCH = 2048
HEAD_DIM = 64
SSD_HEADS = 32
SSD_GROUPS = 4
GROUP_HEADS = SSD_HEADS // SSD_GROUPS
GROUP_WIDTH = GROUP_HEADS * HEAD_DIM
SSD_STATE = 128
SSD_CONV = 5
CHUNK = 128
NA_HEADS = 32
NA_KH = 8
NA_KW = 16
MEM_TOKENS = 256
MEM_HEADS = 4
MEM_HEAD_DIM = 256
MEM_WIDTH = MEM_HEADS * MEM_HEAD_DIM
MIX_WIDTH = BRANCH + MEM_WIDTH
EPS = 1e-6
LOG2E = 1.4426950408889634

LANES = 128
MXU_COLS = 256
PROJ_MAX_TN = 2048
SSD_CHUNKS_PER_STEP = 2
NA_ROWS_PER_STEP = 4
CONV_TILE = 256
CONV_HALO = 8
VMEM_LIMIT = 56 * 1024 * 1024


def _params(*semantics):
    return pltpu.CompilerParams(dimension_semantics=semantics, vmem_limit_bytes=VMEM_LIMIT)


def _silu(v):
    return v * jax.nn.sigmoid(v)


def _softplus(v):
    return jnp.maximum(v, 0.0) + jnp.log1p(jnp.exp(-jnp.abs(v)))


def _split_bf16(v):
    hi = v.astype(BF16)
    lo = (v - hi.astype(F32)).astype(BF16)
    return jnp.concatenate([hi, lo], axis=1)


def _norm_matmul_body(x_ref, g_ref, w_ref, *rest, has_dt):
    if has_dt:
        wdt_ref, dtb_ref, place_ref, o_ref, dt_ref, h_ref = rest
    else:
        o_ref, h_ref = rest

    @pl.when(pl.program_id(1) == 0)
    def _():
        x = x_ref[...]
        ms = jnp.mean(x * x, axis=-1, keepdims=True)
        h = (x * lax.rsqrt(ms + EPS) * g_ref[...]).astype(BF16)
        h_ref[...] = h
        if has_dt:
            dt = _softplus(jnp.dot(h, wdt_ref[...], preferred_element_type=F32) + dtb_ref[...])
            dt_ref[...] = jnp.dot(_split_bf16(dt), place_ref[...], preferred_element_type=F32)

    o_ref[...] = jnp.dot(h_ref[...], w_ref[...], preferred_element_type=F32).astype(o_ref.dtype)


def _norm_matmul(x, g, w, dt_params=None, *, name):
    m, d = x.shape
    wn = w.shape[1]
    tm = min(m, 1024)
    tn = max(t for t in range(MXU_COLS, PROJ_MAX_TN + 1, MXU_COLS) if wn % t == 0)
    in_specs = [
        pl.BlockSpec((tm, d), lambda i, j: (i, 0)),
        pl.BlockSpec((1, d), lambda i, j: (0, 0)),
        pl.BlockSpec((d, tn), lambda i, j: (0, j)),
    ]
    out_shape = [jax.ShapeDtypeStruct((m, wn), BF16)]
    out_specs = [pl.BlockSpec((tm, tn), lambda i, j: (i, j))]
    args = [x, g, w]
    if dt_params is not None:
        for a in dt_params:
            in_specs.append(pl.BlockSpec(a.shape, lambda i, j: (0, 0)))
        dn = dt_params[2].shape[1]
        out_shape.append(jax.ShapeDtypeStruct((m, dn), F32))
        out_specs.append(pl.BlockSpec((tm, dn), lambda i, j: (i, 0)))
        args += list(dt_params)
    outs = pl.pallas_call(
        functools.partial(_norm_matmul_body, has_dt=dt_params is not None),
        grid=(m // tm, wn // tn),
        in_specs=in_specs,
        out_specs=out_specs,
        out_shape=out_shape,
        scratch_shapes=[pltpu.VMEM((tm, d), BF16)],
        compiler_params=_params("parallel", "arbitrary"),
        name=name,
    )(*args)
    return outs if dt_params is not None else outs[0]


def _conv_silu(src_ref, w_ref, b_ref, dst_ref, pad_ref, *, seq, width):
    n_tiles = seq // CONV_TILE
    zeros = jnp.zeros((CONV_HALO, LANES), F32)
    for cc in range(width // LANES):
        cols = slice(cc * LANES, (cc + 1) * LANES)
        pad_ref[0:CONV_HALO, :] = zeros
        pad_ref[seq + CONV_HALO:seq + 2 * CONV_HALO, :] = zeros

        def fill(i, carry, cols=cols):
            r0 = pl.multiple_of(i * CONV_TILE, CONV_TILE)
            dst = pl.multiple_of(r0 + CONV_HALO, CONV_HALO)
            pad_ref[pl.ds(dst, CONV_TILE), :] = src_ref[pl.ds(r0, CONV_TILE), cols].astype(F32)
            return carry

        lax.fori_loop(0, n_tiles, fill, 0)
        w = w_ref[:, cols]
        bias = b_ref[:, cols]

        def tile(i, carry, cols=cols, w=w, bias=bias):
            r0 = pl.multiple_of(i * CONV_TILE, CONV_TILE)
            acc = jnp.broadcast_to(bias, (CONV_TILE, LANES))
            for k in range(SSD_CONV):
                off = CONV_HALO - SSD_CONV // 2 + k
                acc = acc + w[k:k + 1, :] * pad_ref[pl.ds(r0 + off, CONV_TILE), :]
            dst_ref[pl.ds(r0, CONV_TILE), cols] = _silu(acc).astype(BF16)
            return carry

        lax.fori_loop(0, n_tiles, tile, 0)


def _ssd_body(xs_ref, bp_ref, cp_ref, dt_ref, wx_ref, wb_ref, wc_ref, bx_ref, bb_ref, bc_ref,
              alog_ref, dsk_ref, e_ref, y_ref,
              pad_ref, xc_ref, bcv_ref, ccv_ref, part_ref, stf_ref, stb_ref, w_ref, te_ref, fs_ref,
              *, seq):
    n_chunks = seq // CHUNK
    n_steps = n_chunks // SSD_CHUNKS_PER_STEP
    assert n_chunks % (2 * SSD_CHUNKS_PER_STEP) == 0
    _conv_silu(xs_ref, wx_ref, bx_ref, xc_ref, pad_ref, seq=seq, width=GROUP_WIDTH)
    _conv_silu(bp_ref, wb_ref, bb_ref, bcv_ref, pad_ref, seq=seq, width=SSD_STATE)
    _conv_silu(cp_ref, wc_ref, bc_ref, ccv_ref, pad_ref, seq=seq, width=SSD_STATE)

    row = lax.broadcasted_iota(jnp.int32, (CHUNK, CHUNK), 0)
    col = lax.broadcasted_iota(jnp.int32, (CHUNK, CHUNK), 1)
    lane = lax.broadcasted_iota(jnp.int32, (CHUNK, LANES), 1)
    neg_a = -jnp.exp(alog_ref[...])

    def chains_of(step):
        chains = []
        for u in range(SSD_CHUNKS_PER_STEP):
            c = step * SSD_CHUNKS_PER_STEP + u
            chains += [(c, 0, stf_ref), (n_chunks - 1 - c, 1, stb_ref)]
        return [(pl.multiple_of(c * CHUNK, CHUNK), d, st) for c, d, st in chains]

    def keep_mask(direction):
        return (col <= row) if direction == 0 else (col >= row)

    def running_sums(step):
        out = []
        for r0, direction, _ in chains_of(step):
            dt = dt_ref[pl.ds(r0, CHUNK), :]
            tri = jnp.where(keep_mask(direction), 1.0, 0.0).astype(BF16)
            acs2 = jnp.dot(tri, _split_bf16(neg_a * dt), preferred_element_type=F32)
            out.append((dt, acs2[:, :LANES] + acs2[:, LANES:]))
        return out

    def chunk_scalings(step, slot, sums):
        for k, ((r0, direction, _), (dt, acs)) in enumerate(zip(chains_of(step), sums)):
            edge = CHUNK - 1 if direction == 0 else 0
            to_end_dt = jnp.exp(acs[edge:edge + 1, :] - acs) * dt
            scal = jnp.concatenate([to_end_dt, jnp.exp(acs)], axis=0)
            scal_x = jnp.dot(_split_bf16(scal), e_ref[direction], preferred_element_type=F32)
            te_ref[slot, k] = scal_x[0:CHUNK].astype(BF16)
            fs_ref[slot, k] = scal_x[CHUNK:2 * CHUNK]

    def decay_weights(step, slot, sums):
        for k, ((r0, direction, _), (dt, acs)) in enumerate(zip(chains_of(step), sums)):
            cb = lax.dot_general(ccv_ref[pl.ds(r0, CHUNK), :], bcv_ref[pl.ds(r0, CHUNK), :],
                                 (((1,), (1,)), ((), ())), preferred_element_type=F32)
            query = acs * LOG2E
            source_t = (query - jnp.log2(dt)).T
            keep = keep_mask(direction)
            for pair in range(GROUP_HEADS // 2):
                w_pair = []
                for hh in range(2):
                    j = GROUP_HEADS * direction + 2 * pair + hh
                    seg = query[:, j:j + 1] - source_t[j:j + 1, :]
                    w_pair.append((jnp.exp2(jnp.where(keep, seg, -jnp.inf)) * cb).astype(BF16))
                w_ref[slot, k, pair] = jnp.concatenate(w_pair, axis=1)

    def within_chunks(step, slot):
        out = []
        for k, (r0, _, _) in enumerate(chains_of(step)):
            x = xc_ref[pl.ds(r0, CHUNK), :]
            y_parts = []
            for pair in range(GROUP_HEADS // 2):
                xp = x[:, pair * LANES:(pair + 1) * LANES]
                zero = jnp.zeros_like(xp)
                x_bd = jnp.concatenate([jnp.where(lane < HEAD_DIM, xp, zero),
                                        jnp.where(lane >= HEAD_DIM, xp, zero)], axis=0)
                y_parts.append(jnp.dot(w_ref[slot, k, pair], x_bd, preferred_element_type=F32))
            out.append(jnp.concatenate(y_parts, axis=1))
        return out

    def across_chunks(step, slot, y_diag):
        for k, ((r0, direction, st_ref), yd) in enumerate(zip(chains_of(step), y_diag)):
            edge = CHUNK - 1 if direction == 0 else 0
            x = xc_ref[pl.ds(r0, CHUNK), :]
            from_start_x = fs_ref[slot, k]
            s_in = st_ref[...]
            y = yd + jnp.dot(ccv_ref[pl.ds(r0, CHUNK), :], s_in.astype(BF16),
                             preferred_element_type=F32) * from_start_x
            s_new = lax.dot_general(bcv_ref[pl.ds(r0, CHUNK), :], x * te_ref[slot, k],
                                    (((0,), (0,)), ((), ())), preferred_element_type=F32)
            st_ref[...] = s_in * from_start_x[edge:edge + 1, :] + s_new
            if direction == 0:
                y = y + dsk_ref[...] * x.astype(F32)
            y_ref[pl.ds(r0, CHUNK), :] = (part_ref[pl.ds(r0, CHUNK), :] + y).astype(y_ref.dtype)
            part_ref[pl.ds(r0, CHUNK), :] = y

    stf_ref[...] = jnp.zeros_like(stf_ref)
    stb_ref[...] = jnp.zeros_like(stb_ref)

    def clear(i, carry):
        part_ref[pl.ds(pl.multiple_of(i * CHUNK, CHUNK), CHUNK), :] = jnp.zeros((CHUNK, GROUP_WIDTH), F32)
        return carry

    lax.fori_loop(0, n_chunks, clear, 0)

    sums = running_sums(0)
    chunk_scalings(0, 0, sums)
    decay_weights(0, 0, sums)

    def two_steps(j, carry):
        for half in range(2):
            step = 2 * j + half
            nxt = jnp.minimum(step + 1, n_steps - 1)
            y_diag = within_chunks(step, half)
            sums = running_sums(nxt)
            across_chunks(step, half, y_diag)
            chunk_scalings(nxt, 1 - half, sums)
            decay_weights(nxt, 1 - half, sums)
        return carry

    lax.fori_loop(0, n_steps // 2, two_steps, 0)


def _ssd_mixer(proj, dt, p, *, batch, seq):
    rows = batch * seq
    gw, ns = GROUP_WIDTH, SSD_STATE
    xs0 = MIX_WIDTH // gw
    b0 = (MIX_WIDTH + BRANCH) // ns
    c0 = b0 + SSD_GROUPS
    in_specs = [
        pl.BlockSpec((seq, gw), lambda b, g: (b, xs0 + g)),
        pl.BlockSpec((seq, ns), lambda b, g: (b, b0 + g)),
        pl.BlockSpec((seq, ns), lambda b, g: (b, c0 + g)),
        pl.BlockSpec((seq, LANES), lambda b, g: (b, g)),
        pl.BlockSpec((SSD_CONV, gw), lambda b, g: (0, g)),
        pl.BlockSpec((SSD_CONV, ns), lambda b, g: (0, BRANCH // ns + g)),
        pl.BlockSpec((SSD_CONV, ns), lambda b, g: (0, BRANCH // ns + SSD_GROUPS + g)),
        pl.BlockSpec((1, gw), lambda b, g: (0, g)),
        pl.BlockSpec((1, ns), lambda b, g: (0, BRANCH // ns + g)),
        pl.BlockSpec((1, ns), lambda b, g: (0, BRANCH // ns + SSD_GROUPS + g)),
        pl.BlockSpec((1, LANES), lambda b, g: (0, g)),
        pl.BlockSpec((1, gw), lambda b, g: (0, g)),
        pl.BlockSpec((2, 2 * LANES, gw), lambda b, g: (0, 0, 0)),
    ]
    return pl.pallas_call(
        functools.partial(_ssd_body, seq=seq),
        grid=(batch, SSD_GROUPS),
        in_specs=in_specs,
        out_specs=pl.BlockSpec((seq, gw), lambda b, g: (b, g)),
        out_shape=jax.ShapeDtypeStruct((rows, BRANCH), BF16),
        scratch_shapes=[
            pltpu.VMEM((seq + 2 * CONV_HALO, LANES), F32),
            pltpu.VMEM((seq, gw), BF16),
            pltpu.VMEM((seq, ns), BF16),
            pltpu.VMEM((seq, ns), BF16),
            pltpu.VMEM((seq, gw), F32),
            pltpu.VMEM((ns, gw), F32),
            pltpu.VMEM((ns, gw), F32),
            pltpu.VMEM((2, 2 * SSD_CHUNKS_PER_STEP, GROUP_HEADS // 2, CHUNK, 2 * CHUNK), BF16),
            pltpu.VMEM((2, 2 * SSD_CHUNKS_PER_STEP, CHUNK, gw), BF16),
            pltpu.VMEM((2, 2 * SSD_CHUNKS_PER_STEP, CHUNK, gw), F32),
        ],
        compiler_params=_params("parallel", "parallel"),
        name="ssd_mixer",
    )(proj, proj, proj, dt, p["conv_w"], p["conv_w"], p["conv_w"],
      p["conv_b"], p["conv_b"], p["conv_b"], p["a_log"], p["d_skip"], p["expand"])


def _na_body(q_ref, k_ref, v_ref, bias_ref, o_ref, s_ref, *, seq):
    n_rows = seq // GRID_W
    kh = min(NA_KH, n_rows)
    win = kh * GRID_W
    n_steps = n_rows // NA_ROWS_PER_STEP
    assert n_rows % (2 * NA_ROWS_PER_STEP) == 0
    lane = lax.broadcasted_iota(jnp.int32, (GRID_W, LANES), 1)
    lane2 = lax.broadcasted_iota(jnp.int32, (2 * GRID_W, LANES), 1)
    row2 = lax.broadcasted_iota(jnp.int32, (2 * GRID_W, LANES), 0)
    own_channels = (lane2 >= HEAD_DIM) == (row2 >= GRID_W)
    c = HEAD_DIM ** -0.5 * LOG2E

    def window(r):
        rs = jnp.clip(r - kh // 2, 0, n_rows - kh)
        return pl.multiple_of(r * GRID_W, GRID_W), pl.multiple_of(rs * GRID_W, GRID_W), r - rs

    def scores(step, slot):
        for u in range(NA_ROWS_PER_STEP):
            q0, k0, _ = window(step * NA_ROWS_PER_STEP + u)
            q = q_ref[pl.ds(q0, GRID_W), :]
            qq = jnp.concatenate([q, q], axis=0)
            qq = jnp.where(own_channels, qq, jnp.zeros_like(qq))
            s_ref[slot, u] = lax.dot_general(qq, k_ref[pl.ds(k0, win), :], (((1,), (1,)), ((), ())),
                                             preferred_element_type=F32)

    def attend(step, slot):
        probs = []
        for u in range(NA_ROWS_PER_STEP):
            _, _, off = window(step * NA_ROWS_PER_STEP + u)
            s = s_ref[slot, u] * c + bias_ref[off].reshape(2 * GRID_W, win)
            p = jnp.exp2(s - jnp.max(s, axis=-1, keepdims=True))
            probs.append((p.astype(BF16), jnp.sum(p, axis=-1, keepdims=True)))
        for u, (p, denom) in enumerate(probs):
            q0, k0, _ = window(step * NA_ROWS_PER_STEP + u)
            pv = jnp.dot(p, v_ref[pl.ds(k0, win), :], preferred_element_type=F32) / denom
            o = jnp.where(lane < HEAD_DIM, pv[0:GRID_W], pv[GRID_W:2 * GRID_W])
            o_ref[pl.ds(q0, GRID_W), :] = o.astype(o_ref.dtype)

    scores(0, 0)

    def two_steps(j, carry):
        for half in range(2):
            step = 2 * j + half
            scores(jnp.minimum(step + 1, n_steps - 1), 1 - half)
            attend(step, half)
        return carry

    lax.fori_loop(0, n_steps // 2, two_steps, 0)


def _na_mixer(proj, bias_tab, *, batch, seq):
    rows = batch * seq
    q0 = MIX_WIDTH // LANES
    k0 = q0 + BRANCH // LANES
    v0 = k0 + BRANCH // LANES
    n_off, _, _, win = bias_tab.shape
    return pl.pallas_call(
        functools.partial(_na_body, seq=seq),
        grid=(NA_HEADS // 2, batch),
        in_specs=[
            pl.BlockSpec((seq, LANES), lambda hp, b: (b, q0 + hp)),
            pl.BlockSpec((seq, LANES), lambda hp, b: (b, k0 + hp)),
            pl.BlockSpec((seq, LANES), lambda hp, b: (b, v0 + hp)),
            pl.BlockSpec((n_off, 2, GRID_W, win), lambda hp, b: (0, hp, 0, 0)),
        ],
        out_specs=pl.BlockSpec((seq, LANES), lambda hp, b: (b, hp)),
        out_shape=jax.ShapeDtypeStruct((rows, BRANCH), BF16),
        scratch_shapes=[pltpu.VMEM((2, NA_ROWS_PER_STEP, 2 * GRID_W, win), F32)],
        compiler_params=_params("parallel", "parallel"),
        name="na_mixer",
    )(proj, proj, proj, bias_tab)

def _na_bias_table(rpb, n_rows):
    kh = min(NA_KH, n_rows)
    col = jnp.arange(GRID_W)
    col_start = jnp.clip(col - NA_KW // 2, 0, GRID_W - NA_KW)
    col_in = (col[None, :] >= col_start[:, None]) & (col[None, :] < col_start[:, None] + NA_KW)
    dc = jnp.clip(col[None, :] - col[:, None], -(NA_KW - 1), NA_KW - 1) + NA_KW - 1
    by_dc = jnp.take(rpb, dc.reshape(-1), axis=2).reshape(NA_HEADS, 2 * NA_KH - 1, GRID_W, GRID_W)
    by_dc = jnp.where(col_in[None, None], by_dc * LOG2E, -jnp.inf).transpose(0, 2, 1, 3)
    tabs = [by_dc[:, :, NA_KH - 1 - off:NA_KH - 1 - off + kh, :].reshape(NA_HEADS, GRID_W, kh * GRID_W)
            for off in range(kh)]
    return jnp.stack(tabs, axis=0).astype(F32)


def _mem_attn_body(q_ref, z_ref, kv_ref, o_ref):
    scale = MEM_HEAD_DIM ** -0.5
    for h in range(MEM_HEADS):
        cols = slice(h * MEM_HEAD_DIM, (h + 1) * MEM_HEAD_DIM)
        vcols = slice(MEM_WIDTH + h * MEM_HEAD_DIM, MEM_WIDTH + (h + 1) * MEM_HEAD_DIM)
        s = lax.dot_general(q_ref[:, cols], kv_ref[:, cols], (((1,), (1,)), ((), ())),
                            preferred_element_type=F32) * scale
        m = jnp.max(s, axis=-1, keepdims=True)
        p = jnp.exp(s - m)
        denom = jnp.sum(p, axis=-1, keepdims=True)
        o = jnp.dot(p.astype(BF16), kv_ref[:, vcols], preferred_element_type=F32) / denom
        o_ref[:, cols] = (o * _silu(z_ref[:, cols].astype(F32))).astype(o_ref.dtype)


def _mem_attn(proj, kv, *, batch, seq, q_col):
    rows = batch * seq
    tq = min(seq, 512)
    per_seq = seq // tq
    return pl.pallas_call(
        _mem_attn_body,
        grid=(batch, per_seq),
        in_specs=[
            pl.BlockSpec((tq, MEM_WIDTH), lambda b, i: (b * per_seq + i, q_col)),
            pl.BlockSpec((tq, MEM_WIDTH), lambda b, i: (b * per_seq + i, BRANCH // MEM_WIDTH)),
            pl.BlockSpec((MEM_TOKENS, 2 * MEM_WIDTH), lambda b, i: (b, 0)),
        ],
        out_specs=pl.BlockSpec((tq, MEM_WIDTH), lambda b, i: (b * per_seq + i, 0)),
        out_shape=jax.ShapeDtypeStruct((rows, MEM_WIDTH), BF16),
        compiler_params=_params("parallel", "parallel"),
        name="mem_attn",
    )(proj, proj, kv)


def _out_proj_body(y_ref, z_ref, m_ref, x_ref, pg_ref, w_ref, o_ref, *, gated_norm):
    yg = y_ref[...].astype(F32) * _silu(z_ref[...].astype(F32))
    acc = jnp.dot(yg.astype(BF16), w_ref[0:BRANCH, :], preferred_element_type=F32)
    if gated_norm:
        acc = acc * lax.rsqrt(jnp.mean(yg * yg, axis=-1, keepdims=True) + EPS)
    acc = acc + jnp.dot(m_ref[...], w_ref[BRANCH:MIX_WIDTH, :], preferred_element_type=F32)
    ms = jnp.mean(acc * acc, axis=-1, keepdims=True)
    o_ref[...] = x_ref[...] + acc * lax.rsqrt(ms + EPS) * pg_ref[...]


def _out_proj(y, proj, m, x, post_g, w_out, *, gated_norm):
    rows, d = x.shape
    tm = min(rows, 512)
    return pl.pallas_call(
        functools.partial(_out_proj_body, gated_norm=gated_norm),
        grid=(rows // tm,),
        in_specs=[
            pl.BlockSpec((tm, BRANCH), lambda i: (i, 0)),
            pl.BlockSpec((tm, BRANCH), lambda i: (i, 0)),
            pl.BlockSpec((tm, MEM_WIDTH), lambda i: (i, 0)),
            pl.BlockSpec((tm, d), lambda i: (i, 0)),
            pl.BlockSpec((1, d), lambda i: (0, 0)),
            pl.BlockSpec((MIX_WIDTH, d), lambda i: (0, 0)),
        ],
        out_specs=pl.BlockSpec((tm, d), lambda i: (i, 0)),
        out_shape=jax.ShapeDtypeStruct((rows, d), F32),
        compiler_params=_params("parallel"),
        name="out_proj",
    )(y, proj, m, x, post_g, w_out)


def _per_group_lanes(fwd, bwd):
    n = fwd.shape[0]
    f = fwd.reshape(n, SSD_GROUPS, GROUP_HEADS)
    b = bwd.reshape(n, SSD_GROUPS, GROUP_HEADS)
    pad = jnp.zeros((n, SSD_GROUPS, LANES - 2 * GROUP_HEADS), F32)
    return jnp.concatenate([f, b, pad], axis=-1).reshape(n, 1, SSD_GROUPS * LANES)


def _prepare(pre_g, post_g, mem_g, w_mem_kv, w_out, ssd_w_in, ssd_conv_w, ssd_conv_b,
             ssd_dt_bias_f, ssd_dt_bias_b, ssd_a_log_f, ssd_a_log_b, ssd_d, ssd_norm_g,
             na_w_in, na_rpb):
    xbc_end = MIX_WIDTH + BRANCH + 2 * SSD_GROUPS * SSD_STATE
    dt_end = xbc_end + 2 * SSD_HEADS
    n_ssd = ssd_w_in.shape[0]
    dt_pad = LANES - 2 * SSD_HEADS
    w_dt = jnp.pad(ssd_w_in[:, :, xbc_end:dt_end], ((0, 0), (0, 0), (0, dt_pad)))
    dt_bias = jnp.pad(jnp.concatenate([ssd_dt_bias_f, ssd_dt_bias_b], axis=-1), ((0, 0), (0, dt_pad)))
    src = lax.broadcasted_iota(jnp.int32, (2 * LANES, SSD_GROUPS * LANES), 0) % LANES
    dst = lax.broadcasted_iota(jnp.int32, (2 * LANES, SSD_GROUPS * LANES), 1)
    dst_g, dst_l = dst // LANES, dst % LANES
    dt_place = ((dst_l < 2 * GROUP_HEADS)
                & (src == SSD_HEADS * (dst_l // GROUP_HEADS) + GROUP_HEADS * dst_g + dst_l % GROUP_HEADS))
    w_main = jnp.concatenate([ssd_w_in[:, :, :xbc_end], ssd_w_in[:, :, dt_end:]], axis=-1)
    src = lax.broadcasted_iota(jnp.int32, (2, 2 * LANES, GROUP_WIDTH), 1) % LANES
    head = lax.broadcasted_iota(jnp.int32, (2, 2 * LANES, GROUP_WIDTH), 2) // HEAD_DIM
    direction = lax.broadcasted_iota(jnp.int32, (2, 2 * LANES, GROUP_WIDTH), 0)
    expand = (src == GROUP_HEADS * direction + head).astype(BF16)
    gain = jnp.ones((w_out.shape[0], MIX_WIDTH), F32).at[0::2, :BRANCH].set(ssd_norm_g)
    return dict(
        pre_g=pre_g[:, None, :], post_g=post_g[:, None, :], mem_g=mem_g[:, None, :],
        w_mem_kv=w_mem_kv.astype(BF16), w_out=(w_out * gain[:, :, None]).astype(BF16),
        ssd_w_main=w_main.astype(BF16), ssd_w_dt=w_dt.astype(BF16),
        ssd_dt_bias=dt_bias[:, None, :], ssd_dt_place=dt_place.astype(BF16),
        ssd=dict(conv_w=ssd_conv_w, conv_b=ssd_conv_b[:, None, :],
                 a_log=_per_group_lanes(ssd_a_log_f, ssd_a_log_b),
                 d_skip=jnp.repeat(ssd_d, HEAD_DIM, axis=-1)[:, None, :], expand=expand),
        na_w_in=na_w_in.astype(BF16), na_rpb=na_rpb,
    )


def _encoder(x, mem, p):
    batch, seq, d = x.shape
    xf = x.reshape(batch * seq, d)
    memf = mem.reshape(batch * MEM_TOKENS, d)
    bias_tabs = [_na_bias_table(p["na_rpb"][j], seq // GRID_W) for j in range(p["na_rpb"].shape[0])]
    for i in range(DEPTH):
        j = i // 2
        kv = _norm_matmul(memf, p["mem_g"][i], p["w_mem_kv"][i], name="mem_kv")
        if i % 2 == 0:
            dt_params = (p["ssd_w_dt"][j], p["ssd_dt_bias"][j], p["ssd_dt_place"])
            proj, dt = _norm_matmul(xf, p["pre_g"][i], p["ssd_w_main"][j], dt_params, name="ssd_in_proj")
            layer = {k: v[j] for k, v in p["ssd"].items() if k != "expand"}
            layer["expand"] = p["ssd"]["expand"]
            y = _ssd_mixer(proj, dt, layer, batch=batch, seq=seq)
            q_col = (MIX_WIDTH + BRANCH + 2 * SSD_GROUPS * SSD_STATE) // MEM_WIDTH
        else:
            proj = _norm_matmul(xf, p["pre_g"][i], p["na_w_in"][j], name="na_in_proj")
            y = _na_mixer(proj, bias_tabs[j], batch=batch, seq=seq)
            q_col = (MIX_WIDTH + 3 * BRANCH) // MEM_WIDTH
        m = _mem_attn(proj, kv, batch=batch, seq=seq, q_col=q_col)
        xf = _out_proj(y, proj, m, xf, p["post_g"][i], p["w_out"][i], gated_norm=(i % 2 == 0))
    return xf.reshape(batch, seq, d)


def kernel(x_prompt, x_sample, mem_prompt, mem_sample, pre_g, post_g, mem_g, w_mem_kv, w_out, ssd_w_in, ssd_conv_w, ssd_conv_b, ssd_dt_bias_f, ssd_dt_bias_b, ssd_a_log_f, ssd_a_log_b, ssd_d, ssd_norm_g, na_w_in, na_rpb):
    p = _prepare(pre_g, post_g, mem_g, w_mem_kv, w_out, ssd_w_in, ssd_conv_w, ssd_conv_b,
                 ssd_dt_bias_f, ssd_dt_bias_b, ssd_a_log_f, ssd_a_log_b, ssd_d, ssd_norm_g,
                 na_w_in, na_rpb)
    return _encoder(x_prompt, mem_prompt, p), _encoder(x_sample, mem_sample, p)
```

```python
import functools

import jax
import jax.numpy as jnp
from jax import lax
from jax.experimental import pallas as pl
from jax.experimental.pallas import tpu as pltpu

F32 = jnp.float32
BF16 = jnp.bfloat16

DEPTH = 4
D_MODEL = 1024
GRID_W = 64
BRANCH = 2048
HEAD_DIM = 64
SSD_HEADS = 32
SSD_GROUPS = 4
GROUP_HEADS = SSD_HEADS // SSD_GROUPS
GROUP_WIDTH = GROUP_HEADS * HEAD_DIM
SSD_STATE = 128
SSD_CONV = 5
CHUNK = 128
NA_HEADS = 32
NA_KH = 8
NA_KW = 16
MEM_TOKENS = 256
MEM_HEADS = 4
MEM_HEAD_DIM = 256
MEM_WIDTH = MEM_HEADS * MEM_HEAD_DIM
MIX_WIDTH = BRANCH + MEM_WIDTH
EPS = 1e-6
LOG2E = 1.4426950408889634

LANES = 128
MXU_COLS = 256
PROJ_MAX_TN = 2048
SSD_CHUNKS_PER_STEP = 2
NA_ROWS_PER_STEP = 8
CONV_TILE = 256
CONV_HALO = 8
VMEM_LIMIT = 56 * 1024 * 1024


def _params(*semantics):
    return pltpu.CompilerParams(dimension_semantics=semantics, vmem_limit_bytes=VMEM_LIMIT)


def _silu(v):
    return v * jax.nn.sigmoid(v)


def _softplus(v):
    return jnp.maximum(v, 0.0) + jnp.log1p(jnp.exp(-jnp.abs(v)))


def _split_bf16(v):
    hi = v.astype(BF16)
    lo = (v - hi.astype(F32)).astype(BF16)
    return jnp.concatenate([hi, lo], axis=1)


def _norm_matmul_body(x_ref, g_ref, w_ref, *rest, has_dt):
    if has_dt:
        wdt_ref, dtb_ref, place_ref, o_ref, dt_ref, h_ref = rest
    else:
        o_ref, h_ref = rest

    @pl.when(pl.program_id(1) == 0)
    def _():
        x = x_ref[...]
        ms = jnp.mean(x * x, axis=-1, keepdims=True)
        h = (x * lax.rsqrt(ms + EPS) * g_ref[...]).astype(BF16)
        h_ref[...] = h
        if has_dt:
            dt = _softplus(jnp.dot(h, wdt_ref[...], preferred_element_type=F32) + dtb_ref[...])
            dt_ref[...] = jnp.dot(_split_bf16(dt), place_ref[...], preferred_element_type=F32)

    o_ref[...] = jnp.dot(h_ref[...], w_ref[...], preferred_element_type=F32).astype(o_ref.dtype)


def _norm_matmul(x, g, w, dt_params=None, *, name):
    m, d = x.shape
    wn = w.shape[1]
    tm = min(m, 1024)
    tn = max(t for t in range(MXU_COLS, PROJ_MAX_TN + 1, MXU_COLS) if wn % t == 0)
    in_specs = [
        pl.BlockSpec((tm, d), lambda i, j: (i, 0)),
        pl.BlockSpec((1, d), lambda i, j: (0, 0)),
        pl.BlockSpec((d, tn), lambda i, j: (0, j)),
    ]
    out_shape = [jax.ShapeDtypeStruct((m, wn), BF16)]
    out_specs = [pl.BlockSpec((tm, tn), lambda i, j: (i, j))]
    args = [x, g, w]
    if dt_params is not None:
        for a in dt_params:
            in_specs.append(pl.BlockSpec(a.shape, lambda i, j: (0, 0)))
        dn = dt_params[2].shape[1]
        out_shape.append(jax.ShapeDtypeStruct((m, dn), F32))
        out_specs.append(pl.BlockSpec((tm, dn), lambda i, j: (i, 0)))
        args += list(dt_params)
    outs = pl.pallas_call(
        functools.partial(_norm_matmul_body, has_dt=dt_params is not None),
        grid=(m // tm, wn // tn),
        in_specs=in_specs,
        out_specs=out_specs,
        out_shape=out_shape,
        scratch_shapes=[pltpu.VMEM((tm, d), BF16)],
        compiler_params=_params("parallel", "arbitrary"),
        name=name,
    )(*args)
    return outs if dt_params is not None else outs[0]


def _conv_silu(src_ref, w_ref, b_ref, dst_ref, pad_ref, *, seq, width):
    n_tiles = seq // CONV_TILE
    zeros = jnp.zeros((CONV_HALO, LANES), F32)
    for cc in range(width // LANES):
        cols = slice(cc * LANES, (cc + 1) * LANES)
        pad_ref[0:CONV_HALO, :] = zeros
        pad_ref[seq + CONV_HALO:seq + 2 * CONV_HALO, :] = zeros

        def fill(i, carry, cols=cols):
            r0 = pl.multiple_of(i * CONV_TILE, CONV_TILE)
            dst = pl.multiple_of(r0 + CONV_HALO, CONV_HALO)
            pad_ref[pl.ds(dst, CONV_TILE), :] = src_ref[pl.ds(r0, CONV_TILE), cols].astype(F32)
            return carry

        lax.fori_loop(0, n_tiles, fill, 0)
        w = w_ref[:, cols]
        bias = b_ref[:, cols]

        def tile(i, carry, cols=cols, w=w, bias=bias):
            r0 = pl.multiple_of(i * CONV_TILE, CONV_TILE)
            acc = jnp.broadcast_to(bias, (CONV_TILE, LANES))
            for k in range(SSD_CONV):
                off = CONV_HALO - SSD_CONV // 2 + k
                acc = acc + w[k:k + 1, :] * pad_ref[pl.ds(r0 + off, CONV_TILE), :]
            dst_ref[pl.ds(r0, CONV_TILE), cols] = _silu(acc).astype(BF16)
            return carry

        lax.fori_loop(0, n_tiles, tile, 0)


def _ssd_body(xs_ref, bp_ref, cp_ref, dt_ref, wx_ref, wb_ref, wc_ref, bx_ref, bb_ref, bc_ref,
              alog_ref, dsk_ref, e_ref, y_ref,
              pad_ref, xc_ref, bcv_ref, ccv_ref, part_ref, stf_ref, stb_ref, w_ref, te_ref, fs_ref,
              *, seq):
    n_chunks = seq // CHUNK
    n_steps = n_chunks // SSD_CHUNKS_PER_STEP
    assert n_chunks % (2 * SSD_CHUNKS_PER_STEP) == 0
    _conv_silu(xs_ref, wx_ref, bx_ref, xc_ref, pad_ref, seq=seq, width=GROUP_WIDTH)
    _conv_silu(bp_ref, wb_ref, bb_ref, bcv_ref, pad_ref, seq=seq, width=SSD_STATE)
    _conv_silu(cp_ref, wc_ref, bc_ref, ccv_ref, pad_ref, seq=seq, width=SSD_STATE)

    row = lax.broadcasted_iota(jnp.int32, (CHUNK, CHUNK), 0)
    col = lax.broadcasted_iota(jnp.int32, (CHUNK, CHUNK), 1)
    lane = lax.broadcasted_iota(jnp.int32, (CHUNK, LANES), 1)
    neg_a = -jnp.exp(alog_ref[...])

    def chains_of(step):
        chains = []
        for u in range(SSD_CHUNKS_PER_STEP):
            c = step * SSD_CHUNKS_PER_STEP + u
            chains += [(c, 0, stf_ref), (n_chunks - 1 - c, 1, stb_ref)]
        return [(pl.multiple_of(c * CHUNK, CHUNK), d, st) for c, d, st in chains]

    def keep_mask(direction):
        return (col <= row) if direction == 0 else (col >= row)

    def running_sums(step):
        out = []
        for r0, direction, _ in chains_of(step):
            dt = dt_ref[pl.ds(r0, CHUNK), :]
            tri = jnp.where(keep_mask(direction), 1.0, 0.0).astype(BF16)
            acs2 = jnp.dot(tri, _split_bf16(neg_a * dt), preferred_element_type=F32)
            out.append((dt, acs2[:, :LANES] + acs2[:, LANES:]))
        return out

    def chunk_scalings(step, slot, sums):
        for k, ((r0, direction, _), (dt, acs)) in enumerate(zip(chains_of(step), sums)):
            edge = CHUNK - 1 if direction == 0 else 0
            to_end_dt = jnp.exp(acs[edge:edge + 1, :] - acs) * dt
            scal = jnp.concatenate([to_end_dt, jnp.exp(acs)], axis=0)
            scal_x = jnp.dot(_split_bf16(scal), e_ref[direction], preferred_element_type=F32)
            te_ref[slot, k] = scal_x[0:CHUNK].astype(BF16)
            fs_ref[slot, k] = scal_x[CHUNK:2 * CHUNK]

    def decay_weights(step, slot, sums):
        for k, ((r0, direction, _), (dt, acs)) in enumerate(zip(chains_of(step), sums)):
            cb = lax.dot_general(ccv_ref[pl.ds(r0, CHUNK), :], bcv_ref[pl.ds(r0, CHUNK), :],
                                 (((1,), (1,)), ((), ())), preferred_element_type=F32)
            query = acs * LOG2E
            source_t = (query - jnp.log2(dt)).T
            keep = keep_mask(direction)
            for pair in range(GROUP_HEADS // 2):
                w_pair = []
                for hh in range(2):
                    j = GROUP_HEADS * direction + 2 * pair + hh
                    seg = query[:, j:j + 1] - source_t[j:j + 1, :]
                    w_pair.append((jnp.exp2(jnp.where(keep, seg, -jnp.inf)) * cb).astype(BF16))
                w_ref[slot, k, pair] = jnp.concatenate(w_pair, axis=1)

    def within_chunks(step, slot):
        out = []
        for k, (r0, _, _) in enumerate(chains_of(step)):
            x = xc_ref[pl.ds(r0, CHUNK), :]
            y_parts = []
            for pair in range(GROUP_HEADS // 2):
                xp = x[:, pair * LANES:(pair + 1) * LANES]
                zero = jnp.zeros_like(xp)
                x_bd = jnp.concatenate([jnp.where(lane < HEAD_DIM, xp, zero),
                                        jnp.where(lane >= HEAD_DIM, xp, zero)], axis=0)
                y_parts.append(jnp.dot(w_ref[slot, k, pair], x_bd, preferred_element_type=F32))
            out.append(jnp.concatenate(y_parts, axis=1))
        return out

    def across_chunks(step, slot, y_diag):
        for k, ((r0, direction, st_ref), yd) in enumerate(zip(chains_of(step), y_diag)):
            edge = CHUNK - 1 if direction == 0 else 0
            x = xc_ref[pl.ds(r0, CHUNK), :]
            from_start_x = fs_ref[slot, k]
            s_in = st_ref[...]
            y = yd + jnp.dot(ccv_ref[pl.ds(r0, CHUNK), :], s_in.astype(BF16),
                             preferred_element_type=F32) * from_start_x
            s_new = lax.dot_general(bcv_ref[pl.ds(r0, CHUNK), :], x * te_ref[slot, k],
                                    (((0,), (0,)), ((), ())), preferred_element_type=F32)
            st_ref[...] = s_in * from_start_x[edge:edge + 1, :] + s_new
            if direction == 0:
                y = y + dsk_ref[...] * x.astype(F32)
            y_ref[pl.ds(r0, CHUNK), :] = (part_ref[pl.ds(r0, CHUNK), :] + y).astype(y_ref.dtype)
            part_ref[pl.ds(r0, CHUNK), :] = y

    stf_ref[...] = jnp.zeros_like(stf_ref)
    stb_ref[...] = jnp.zeros_like(stb_ref)

    def clear(i, carry):
        part_ref[pl.ds(pl.multiple_of(i * CHUNK, CHUNK), CHUNK), :] = jnp.zeros((CHUNK, GROUP_WIDTH), F32)
        return carry

    lax.fori_loop(0, n_chunks, clear, 0)

    sums = running_sums(0)
    chunk_scalings(0, 0, sums)
    decay_weights(0, 0, sums)

    def two_steps(j, carry):
        for half in range(2):
            step = 2 * j + half
            nxt = jnp.minimum(step + 1, n_steps - 1)
            y_diag = within_chunks(step, half)
            sums = running_sums(nxt)
            across_chunks(step, half, y_diag)
            chunk_scalings(nxt, 1 - half, sums)
            decay_weights(nxt, 1 - half, sums)
        return carry

    lax.fori_loop(0, n_steps // 2, two_steps, 0)


def _ssd_mixer(proj, dt, p, *, batch, seq):
    rows = batch * seq
    gw, ns = GROUP_WIDTH, SSD_STATE
    xs0 = MIX_WIDTH // gw
    b0 = (MIX_WIDTH + BRANCH) // ns
    c0 = b0 + SSD_GROUPS
    in_specs = [
        pl.BlockSpec((seq, gw), lambda b, g: (b, xs0 + g)),
        pl.BlockSpec((seq, ns), lambda b, g: (b, b0 + g)),
        pl.BlockSpec((seq, ns), lambda b, g: (b, c0 + g)),
        pl.BlockSpec((seq, LANES), lambda b, g: (b, g)),
        pl.BlockSpec((SSD_CONV, gw), lambda b, g: (0, g)),
        pl.BlockSpec((SSD_CONV, ns), lambda b, g: (0, BRANCH // ns + g)),
        pl.BlockSpec((SSD_CONV, ns), lambda b, g: (0, BRANCH // ns + SSD_GROUPS + g)),
        pl.BlockSpec((1, gw), lambda b, g: (0, g)),
        pl.BlockSpec((1, ns), lambda b, g: (0, BRANCH // ns + g)),
        pl.BlockSpec((1, ns), lambda b, g: (0, BRANCH // ns + SSD_GROUPS + g)),
        pl.BlockSpec((1, LANES), lambda b, g: (0, g)),
        pl.BlockSpec((1, gw), lambda b, g: (0, g)),
        pl.BlockSpec((2, 2 * LANES, gw), lambda b, g: (0, 0, 0)),
    ]
    return pl.pallas_call(
        functools.partial(_ssd_body, seq=seq),
        grid=(batch, SSD_GROUPS),
        in_specs=in_specs,
        out_specs=pl.BlockSpec((seq, gw), lambda b, g: (b, g)),
        out_shape=jax.ShapeDtypeStruct((rows, BRANCH), BF16),
        scratch_shapes=[
            pltpu.VMEM((seq + 2 * CONV_HALO, LANES), F32),
            pltpu.VMEM((seq, gw), BF16),
            pltpu.VMEM((seq, ns), BF16),
            pltpu.VMEM((seq, ns), BF16),
            pltpu.VMEM((seq, gw), F32),
            pltpu.VMEM((ns, gw), F32),
            pltpu.VMEM((ns, gw), F32),
            pltpu.VMEM((2, 2 * SSD_CHUNKS_PER_STEP, GROUP_HEADS // 2, CHUNK, 2 * CHUNK), BF16),
            pltpu.VMEM((2, 2 * SSD_CHUNKS_PER_STEP, CHUNK, gw), BF16),
            pltpu.VMEM((2, 2 * SSD_CHUNKS_PER_STEP, CHUNK, gw), F32),
        ],
        compiler_params=_params("parallel", "parallel"),
        name="ssd_mixer",
    )(proj, proj, proj, dt, p["conv_w"], p["conv_w"], p["conv_w"],
      p["conv_b"], p["conv_b"], p["conv_b"], p["a_log"], p["d_skip"], p["expand"])


def _na_body(q_ref, k_ref, v_ref, bias_ref, o_ref, *, seq):
    n_rows = seq // GRID_W
    kh = min(NA_KH, n_rows)
    win = kh * GRID_W
    assert n_rows % NA_ROWS_PER_STEP == 0
    lane = lax.broadcasted_iota(jnp.int32, (GRID_W, LANES), 1)
    lane2 = lax.broadcasted_iota(jnp.int32, (2 * GRID_W, LANES), 1)
    row2 = lax.broadcasted_iota(jnp.int32, (2 * GRID_W, LANES), 0)
    own_channels = (lane2 >= HEAD_DIM) == (row2 >= GRID_W)

    def window(r):
        rs = jnp.clip(r - kh // 2, 0, n_rows - kh)
        return pl.multiple_of(r * GRID_W, GRID_W), pl.multiple_of(rs * GRID_W, GRID_W), r - rs

    def rows_step(i, carry):
        rows = [window(i * NA_ROWS_PER_STEP + u) for u in range(NA_ROWS_PER_STEP)]
        scores = []
        for q0, k0, _ in rows:
            q = q_ref[pl.ds(q0, GRID_W), :]
            qq = jnp.concatenate([q, q], axis=0)
            qq = jnp.where(own_channels, qq, jnp.zeros_like(qq))
            scores.append(lax.dot_general(qq, k_ref[pl.ds(k0, win), :], (((1,), (1,)), ((), ())),
                                          preferred_element_type=F32))
        probs = []
        for (_, _, off), s in zip(rows, scores):
            s = s + bias_ref[off].reshape(2 * GRID_W, win)
            p = jnp.exp2(s - jnp.max(s, axis=-1, keepdims=True))
            probs.append((p.astype(BF16), jnp.sum(p, axis=-1, keepdims=True)))
        for (q0, k0, _), (p, denom) in zip(rows, probs):
            pv = jnp.dot(p, v_ref[pl.ds(k0, win), :], preferred_element_type=F32) / denom
            o = jnp.where(lane < HEAD_DIM, pv[0:GRID_W], pv[GRID_W:2 * GRID_W])
            o_ref[pl.ds(q0, GRID_W), :] = o.astype(o_ref.dtype)
        return carry

    lax.fori_loop(0, n_rows // NA_ROWS_PER_STEP, rows_step, 0)


def _na_mixer(proj, bias_tab, *, batch, seq):
    rows = batch * seq
    q0 = MIX_WIDTH // LANES
    k0 = q0 + BRANCH // LANES
    v0 = k0 + BRANCH // LANES
    n_off, _, _, win = bias_tab.shape
    return pl.pallas_call(
        functools.partial(_na_body, seq=seq),
        grid=(NA_HEADS // 2, batch),
        in_specs=[
            pl.BlockSpec((seq, LANES), lambda hp, b: (b, q0 + hp)),
            pl.BlockSpec((seq, LANES), lambda hp, b: (b, k0 + hp)),
            pl.BlockSpec((seq, LANES), lambda hp, b: (b, v0 + hp)),
            pl.BlockSpec((n_off, 2, GRID_W, win), lambda hp, b: (0, hp, 0, 0)),
        ],
        out_specs=pl.BlockSpec((seq, LANES), lambda hp, b: (b, hp)),
        out_shape=jax.ShapeDtypeStruct((rows, BRANCH), BF16),
        compiler_params=_params("parallel", "parallel"),
        name="na_mixer",
    )(proj, proj, proj, bias_tab)

def _na_bias_table(rpb, n_rows):
    kh = min(NA_KH, n_rows)
    col = jnp.arange(GRID_W)
    col_start = jnp.clip(col - NA_KW // 2, 0, GRID_W - NA_KW)
    col_in = (col[None, :] >= col_start[:, None]) & (col[None, :] < col_start[:, None] + NA_KW)
    dc = jnp.clip(col[None, :] - col[:, None], -(NA_KW - 1), NA_KW - 1) + NA_KW - 1
    by_dc = jnp.take(rpb, dc.reshape(-1), axis=2).reshape(NA_HEADS, 2 * NA_KH - 1, GRID_W, GRID_W)
    by_dc = jnp.where(col_in[None, None], by_dc * LOG2E, -jnp.inf).transpose(0, 2, 1, 3)
    tabs = [by_dc[:, :, NA_KH - 1 - off:NA_KH - 1 - off + kh, :].reshape(NA_HEADS, GRID_W, kh * GRID_W)
            for off in range(kh)]
    return jnp.stack(tabs, axis=0).astype(F32)


def _out_proj_body(y_ref, z_ref, q_ref, kv_ref, x_ref, pg_ref, w_ref, o_ref, *, gated_norm):
    heads = [slice(h * MEM_HEAD_DIM, (h + 1) * MEM_HEAD_DIM) for h in range(MEM_HEADS)]
    scores = [lax.dot_general(q_ref[:, c], kv_ref[:, c], (((1,), (1,)), ((), ())),
                              preferred_element_type=F32) for c in heads]
    yg = y_ref[...].astype(F32) * _silu(z_ref[:, 0:BRANCH].astype(F32))
    acc = jnp.dot(yg.astype(BF16), w_ref[0:BRANCH, :], preferred_element_type=F32)
    if gated_norm:
        acc = acc * lax.rsqrt(jnp.mean(yg * yg, axis=-1, keepdims=True) + EPS)
    for c, s in zip(heads, scores):
        p = jnp.exp2(s - jnp.max(s, axis=-1, keepdims=True))
        denom = jnp.sum(p, axis=-1, keepdims=True)
        vcols = slice(MEM_WIDTH + c.start, MEM_WIDTH + c.stop)
        mem = jnp.dot(p.astype(BF16), kv_ref[:, vcols], preferred_element_type=F32) / denom
        gate = _silu(z_ref[:, BRANCH + c.start:BRANCH + c.stop].astype(F32))
        acc = acc + jnp.dot((mem * gate).astype(BF16), w_ref[BRANCH + c.start:BRANCH + c.stop, :],
                            preferred_element_type=F32)
    ms = jnp.mean(acc * acc, axis=-1, keepdims=True)
    o_ref[...] = x_ref[...] + acc * lax.rsqrt(ms + EPS) * pg_ref[...]


def _out_proj(y, proj, kv, x, post_g, w_out, *, seq, q_col, gated_norm):
    rows, d = x.shape
    tm = min(seq, 512)
    per_seq = seq // tm
    return pl.pallas_call(
        functools.partial(_out_proj_body, gated_norm=gated_norm),
        grid=(rows // tm,),
        in_specs=[
            pl.BlockSpec((tm, BRANCH), lambda i: (i, 0)),
            pl.BlockSpec((tm, MIX_WIDTH), lambda i: (i, 0)),
            pl.BlockSpec((tm, MEM_WIDTH), lambda i: (i, q_col)),
            pl.BlockSpec((MEM_TOKENS, 2 * MEM_WIDTH), lambda i: (i // per_seq, 0)),
            pl.BlockSpec((tm, d), lambda i: (i, 0)),
            pl.BlockSpec((1, d), lambda i: (0, 0)),
            pl.BlockSpec((MIX_WIDTH, d), lambda i: (0, 0)),
        ],
        out_specs=pl.BlockSpec((tm, d), lambda i: (i, 0)),
        out_shape=jax.ShapeDtypeStruct((rows, d), F32),
        compiler_params=_params("parallel"),
        name="out_proj",
    )(y, proj, proj, kv, x, post_g, w_out)


def _per_group_lanes(fwd, bwd):
    n = fwd.shape[0]
    f = fwd.reshape(n, SSD_GROUPS, GROUP_HEADS)
    b = bwd.reshape(n, SSD_GROUPS, GROUP_HEADS)
    pad = jnp.zeros((n, SSD_GROUPS, LANES - 2 * GROUP_HEADS), F32)
    return jnp.concatenate([f, b, pad], axis=-1).reshape(n, 1, SSD_GROUPS * LANES)


def _prepare(pre_g, post_g, mem_g, w_mem_kv, w_out, ssd_w_in, ssd_conv_w, ssd_conv_b,
             ssd_dt_bias_f, ssd_dt_bias_b, ssd_a_log_f, ssd_a_log_b, ssd_d, ssd_norm_g,
             na_w_in, na_rpb):
    xbc_end = MIX_WIDTH + BRANCH + 2 * SSD_GROUPS * SSD_STATE
    dt_end = xbc_end + 2 * SSD_HEADS
    n_ssd = ssd_w_in.shape[0]
    dt_pad = LANES - 2 * SSD_HEADS
    w_dt = jnp.pad(ssd_w_in[:, :, xbc_end:dt_end], ((0, 0), (0, 0), (0, dt_pad)))
    dt_bias = jnp.pad(jnp.concatenate([ssd_dt_bias_f, ssd_dt_bias_b], axis=-1), ((0, 0), (0, dt_pad)))
    src = lax.broadcasted_iota(jnp.int32, (2 * LANES, SSD_GROUPS * LANES), 0) % LANES
    dst = lax.broadcasted_iota(jnp.int32, (2 * LANES, SSD_GROUPS * LANES), 1)
    dst_g, dst_l = dst // LANES, dst % LANES
    dt_place = ((dst_l < 2 * GROUP_HEADS)
                & (src == SSD_HEADS * (dst_l // GROUP_HEADS) + GROUP_HEADS * dst_g + dst_l % GROUP_HEADS))
    mem_c = MEM_HEAD_DIM ** -0.5 * LOG2E
    w_main = jnp.concatenate([ssd_w_in[:, :, :xbc_end], ssd_w_in[:, :, dt_end:] * mem_c], axis=-1)
    na_cols = jnp.arange(na_w_in.shape[-1])
    na_c = jnp.where((na_cols >= MIX_WIDTH) & (na_cols < MIX_WIDTH + BRANCH), HEAD_DIM ** -0.5 * LOG2E,
                     jnp.where(na_cols >= MIX_WIDTH + 3 * BRANCH, mem_c, 1.0)).astype(F32)
    src = lax.broadcasted_iota(jnp.int32, (2, 2 * LANES, GROUP_WIDTH), 1) % LANES
    head = lax.broadcasted_iota(jnp.int32, (2, 2 * LANES, GROUP_WIDTH), 2) // HEAD_DIM
    direction = lax.broadcasted_iota(jnp.int32, (2, 2 * LANES, GROUP_WIDTH), 0)
    expand = (src == GROUP_HEADS * direction + head).astype(BF16)
    gain = jnp.ones((w_out.shape[0], MIX_WIDTH), F32).at[0::2, :BRANCH].set(ssd_norm_g)
    return dict(
        pre_g=pre_g[:, None, :], post_g=post_g[:, None, :], mem_g=mem_g[:, None, :],
        w_mem_kv=w_mem_kv.astype(BF16), w_out=(w_out * gain[:, :, None]).astype(BF16),
        ssd_w_main=w_main.astype(BF16), ssd_w_dt=w_dt.astype(BF16),
        ssd_dt_bias=dt_bias[:, None, :], ssd_dt_place=dt_place.astype(BF16),
        ssd=dict(conv_w=ssd_conv_w, conv_b=ssd_conv_b[:, None, :],
                 a_log=_per_group_lanes(ssd_a_log_f, ssd_a_log_b),
                 d_skip=jnp.repeat(ssd_d, HEAD_DIM, axis=-1)[:, None, :], expand=expand),
        na_w_in=(na_w_in * na_c).astype(BF16), na_rpb=na_rpb,
    )


def _encoder(x, mem, p):
    batch, seq, d = x.shape
    xf = x.reshape(batch * seq, d)
    memf = mem.reshape(batch * MEM_TOKENS, d)
    bias_tabs = [_na_bias_table(p["na_rpb"][j], seq // GRID_W) for j in range(p["na_rpb"].shape[0])]
    for i in range(DEPTH):
        j = i // 2
        kv = _norm_matmul(memf, p["mem_g"][i], p["w_mem_kv"][i], name="mem_kv")
        if i % 2 == 0:
            dt_params = (p["ssd_w_dt"][j], p["ssd_dt_bias"][j], p["ssd_dt_place"])
            proj, dt = _norm_matmul(xf, p["pre_g"][i], p["ssd_w_main"][j], dt_params, name="ssd_in_proj")
            layer = {k: v[j] for k, v in p["ssd"].items() if k != "expand"}
            layer["expand"] = p["ssd"]["expand"]
            y = _ssd_mixer(proj, dt, layer, batch=batch, seq=seq)
            q_col = (MIX_WIDTH + BRANCH + 2 * SSD_GROUPS * SSD_STATE) // MEM_WIDTH
        else:
            proj = _norm_matmul(xf, p["pre_g"][i], p["na_w_in"][j], name="na_in_proj")
            y = _na_mixer(proj, bias_tabs[j], batch=batch, seq=seq)
            q_col = (MIX_WIDTH + 3 * BRANCH) // MEM_WIDTH
        xf = _out_proj(y, proj, kv, xf, p["post_g"][i], p["w_out"][i], seq=seq, q_col=q_col,
                       gated_norm=(i % 2 == 0))
    return xf.reshape(batch, seq, d)


def kernel(x_prompt, x_sample, mem_prompt, mem_sample, pre_g, post_g, mem_g, w_mem_kv, w_out, ssd_w_in, ssd_conv_w, ssd_conv_b, ssd_dt_bias_f, ssd_dt_bias_b, ssd_a_log_f, ssd_a_log_b, ssd_d, ssd_norm_g, na_w_in, na_rpb):
    p = _prepare(pre_g, post_g, mem_g, w_mem_kv, w_out, ssd_w_in, ssd_conv_w, ssd_conv_b,
                 ssd_dt_bias_f, ssd_dt_bias_b, ssd_a_log_f, ssd_a_log_b, ssd_d, ssd_norm_g,
                 na_w_in, na_rpb)
    return _encoder(x_prompt, mem_prompt, p), _encoder(x_sample, mem_sample, p)
```

```python
import functools

import jax
import jax.numpy as jnp
from jax import lax
from jax.experimental import pallas as pl
from jax.experimental.pallas import tpu as pltpu

F32 = jnp.float32
BF16 = jnp.bfloat16

DEPTH = 4
D_MODEL = 1024
GRID_W = 64
BRANCH = 2048
HEAD_DIM = 64
SSD_HEADS = 32
SSD_GROUPS = 4
GROUP_HEADS = SSD_HEADS // SSD_GROUPS
GROUP_WIDTH = GROUP_HEADS * HEAD_DIM
SSD_STATE = 128
SSD_CONV = 5
CHUNK = 128
NA_HEADS = 32
NA_KH = 8
NA_KW = 16
MEM_TOKENS = 256
MEM_HEADS = 4
MEM_HEAD_DIM = 256
MEM_WIDTH = MEM_HEADS * MEM_HEAD_DIM
MIX_WIDTH = BRANCH + MEM_WIDTH
EPS = 1e-6
LOG2E = 1.4426950408889634

LANES = 128
MXU_COLS = 256
PROJ_MAX_TN = 2048
PROJ_TM = 2048
SSD_CHUNKS_PER_STEP = 2
NA_ROWS_PER_STEP = 16
CONV_TILE = 256
CONV_HALO = 8
VMEM_LIMIT = 56 * 1024 * 1024


def _params(*semantics):
    return pltpu.CompilerParams(dimension_semantics=semantics, vmem_limit_bytes=VMEM_LIMIT)


def _silu(v):
    return v * jax.nn.sigmoid(v)


def _softplus(v):
    return jnp.maximum(v, 0.0) + jnp.log1p(jnp.exp(-jnp.abs(v)))


def _split_bf16(v):
    hi = v.astype(BF16)
    lo = (v - hi.astype(F32)).astype(BF16)
    return jnp.concatenate([hi, lo], axis=1)


def _norm_matmul_body(x_ref, g_ref, w_ref, *rest, has_dt):
    if has_dt:
        wdt_ref, dtb_ref, place_ref, o_ref, dt_ref, h_ref = rest
    else:
        o_ref, h_ref = rest

    @pl.when(pl.program_id(1) == 0)
    def _():
        x = x_ref[...]
        ms = jnp.mean(x * x, axis=-1, keepdims=True)
        h = (x * lax.rsqrt(ms + EPS) * g_ref[...]).astype(BF16)
        h_ref[...] = h
        if has_dt:
            dt = _softplus(jnp.dot(h, wdt_ref[...], preferred_element_type=F32) + dtb_ref[...])
            dt_ref[...] = jnp.dot(_split_bf16(dt), place_ref[...], preferred_element_type=F32)

    o_ref[...] = jnp.dot(h_ref[...], w_ref[...], preferred_element_type=F32).astype(o_ref.dtype)


def _norm_matmul(x, g, w, dt_params=None, *, name):
    m, d = x.shape
    wn = w.shape[1]
    tm = min(m, PROJ_TM if dt_params is None else PROJ_TM // 2)
    tn = max(t for t in range(MXU_COLS, PROJ_MAX_TN + 1, MXU_COLS) if wn % t == 0)
    in_specs = [
        pl.BlockSpec((tm, d), lambda i, j: (i, 0)),
        pl.BlockSpec((1, d), lambda i, j: (0, 0)),
        pl.BlockSpec((d, tn), lambda i, j: (0, j)),
    ]
    out_shape = [jax.ShapeDtypeStruct((m, wn), BF16)]
    out_specs = [pl.BlockSpec((tm, tn), lambda i, j: (i, j))]
    args = [x, g, w]
    if dt_params is not None:
        for a in dt_params:
            in_specs.append(pl.BlockSpec(a.shape, lambda i, j: (0, 0)))
        dn = dt_params[2].shape[1]
        out_shape.append(jax.ShapeDtypeStruct((m, dn), F32))
        out_specs.append(pl.BlockSpec((tm, dn), lambda i, j: (i, 0)))
        args += list(dt_params)
    outs = pl.pallas_call(
        functools.partial(_norm_matmul_body, has_dt=dt_params is not None),
        grid=(m // tm, wn // tn),
        in_specs=in_specs,
        out_specs=out_specs,
        out_shape=out_shape,
        scratch_shapes=[pltpu.VMEM((tm, d), BF16)],
        compiler_params=_params("parallel", "arbitrary"),
        name=name,
    )(*args)
    return outs if dt_params is not None else outs[0]


def _conv_silu(src_ref, w_ref, b_ref, dst_ref, pad_ref, *, seq, width):
    n_tiles = seq // CONV_TILE
    zeros = jnp.zeros((CONV_HALO, LANES), F32)
    for cc in range(width // LANES):
        cols = slice(cc * LANES, (cc + 1) * LANES)
        pad_ref[0:CONV_HALO, :] = zeros
        pad_ref[seq + CONV_HALO:seq + 2 * CONV_HALO, :] = zeros

        def fill(i, carry, cols=cols):
            r0 = pl.multiple_of(i * CONV_TILE, CONV_TILE)
            dst = pl.multiple_of(r0 + CONV_HALO, CONV_HALO)
            pad_ref[pl.ds(dst, CONV_TILE), :] = src_ref[pl.ds(r0, CONV_TILE), cols].astype(F32)
            return carry

        lax.fori_loop(0, n_tiles, fill, 0)
        w = w_ref[:, cols]
        bias = b_ref[:, cols]

        def tile(i, carry, cols=cols, w=w, bias=bias):
            r0 = pl.multiple_of(i * CONV_TILE, CONV_TILE)
            acc = jnp.broadcast_to(bias, (CONV_TILE, LANES))
            for k in range(SSD_CONV):
                off = CONV_HALO - SSD_CONV // 2 + k
                acc = acc + w[k:k + 1, :] * pad_ref[pl.ds(r0 + off, CONV_TILE), :]
            dst_ref[pl.ds(r0, CONV_TILE), cols] = _silu(acc).astype(BF16)
            return carry

        lax.fori_loop(0, n_tiles, tile, 0)


def _ssd_body(xs_ref, bp_ref, cp_ref, dt_ref, wx_ref, wb_ref, wc_ref, bx_ref, bb_ref, bc_ref,
              alog_ref, dsk_ref, e_ref, y_ref,
              pad_ref, xc_ref, bcv_ref, ccv_ref, part_ref, stf_ref, stb_ref, w_ref, te_ref, fs_ref,
              *, seq):
    n_chunks = seq // CHUNK
    n_steps = n_chunks // SSD_CHUNKS_PER_STEP
    assert n_chunks % (4 * SSD_CHUNKS_PER_STEP) == 0
    _conv_silu(xs_ref, wx_ref, bx_ref, xc_ref, pad_ref, seq=seq, width=GROUP_WIDTH)
    _conv_silu(bp_ref, wb_ref, bb_ref, bcv_ref, pad_ref, seq=seq, width=SSD_STATE)
    _conv_silu(cp_ref, wc_ref, bc_ref, ccv_ref, pad_ref, seq=seq, width=SSD_STATE)

    row = lax.broadcasted_iota(jnp.int32, (CHUNK, CHUNK), 0)
    col = lax.broadcasted_iota(jnp.int32, (CHUNK, CHUNK), 1)
    lane = lax.broadcasted_iota(jnp.int32, (CHUNK, LANES), 1)
    neg_a = -jnp.exp(alog_ref[...])

    def chains_of(step):
        chains = []
        for u in range(SSD_CHUNKS_PER_STEP):
            c = step * SSD_CHUNKS_PER_STEP + u
            chains += [(c, 0, stf_ref), (n_chunks - 1 - c, 1, stb_ref)]
        return [(pl.multiple_of(c * CHUNK, CHUNK), d, st) for c, d, st in chains]

    def keep_mask(direction):
        return (col <= row) if direction == 0 else (col >= row)

    def running_sums(step):
        out = []
        for r0, direction, _ in chains_of(step):
            dt = dt_ref[pl.ds(r0, CHUNK), :]
            tri = jnp.where(keep_mask(direction), 1.0, 0.0).astype(BF16)
            acs2 = jnp.dot(tri, _split_bf16(neg_a * dt), preferred_element_type=F32)
            out.append((dt, acs2[:, :LANES] + acs2[:, LANES:]))
        return out

    def chunk_scalings(step, slot, sums):
        for k, ((r0, direction, _), (dt, acs)) in enumerate(zip(chains_of(step), sums)):
            edge = CHUNK - 1 if direction == 0 else 0
            to_end_dt = jnp.exp(acs[edge:edge + 1, :] - acs) * dt
            scal = jnp.concatenate([to_end_dt, jnp.exp(acs)], axis=0)
            scal_x = jnp.dot(_split_bf16(scal), e_ref[direction], preferred_element_type=F32)
            te_ref[slot, k] = scal_x[0:CHUNK].astype(BF16)
            fs_ref[slot, k] = scal_x[CHUNK:2 * CHUNK]

    def decay_weights(step, slot, sums):
        for k, ((r0, direction, _), (dt, acs)) in enumerate(zip(chains_of(step), sums)):
            cb = lax.dot_general(ccv_ref[pl.ds(r0, CHUNK), :], bcv_ref[pl.ds(r0, CHUNK), :],
                                 (((1,), (1,)), ((), ())), preferred_element_type=F32)
            query = acs * LOG2E
            source_t = (query - jnp.log2(dt)).T
            keep = keep_mask(direction)
            for pair in range(GROUP_HEADS // 2):
                w_pair = []
                for hh in range(2):
                    j = GROUP_HEADS * direction + 2 * pair + hh
                    seg = query[:, j:j + 1] - source_t[j:j + 1, :]
                    w_pair.append((jnp.exp2(jnp.where(keep, seg, -jnp.inf)) * cb).astype(BF16))
                w_ref[slot, k, pair] = jnp.concatenate(w_pair, axis=1)

    def within_chunks(step, slot):
        out = []
        for k, (r0, _, _) in enumerate(chains_of(step)):
            x = xc_ref[pl.ds(r0, CHUNK), :]
            y_parts = []
            for pair in range(GROUP_HEADS // 2):
                xp = x[:, pair * LANES:(pair + 1) * LANES]
                zero = jnp.zeros_like(xp)
                x_bd = jnp.concatenate([jnp.where(lane < HEAD_DIM, xp, zero),
                                        jnp.where(lane >= HEAD_DIM, xp, zero)], axis=0)
                y_parts.append(jnp.dot(w_ref[slot, k, pair], x_bd, preferred_element_type=F32))
            out.append(jnp.concatenate(y_parts, axis=1))
        return out

    def across_chunks(step, slot, y_diag, second_visit):
        for k, ((r0, direction, st_ref), yd) in enumerate(zip(chains_of(step), y_diag)):
            edge = CHUNK - 1 if direction == 0 else 0
            x = xc_ref[pl.ds(r0, CHUNK), :]
            from_start_x = fs_ref[slot, k]
            s_in = st_ref[...]
            y = yd + jnp.dot(ccv_ref[pl.ds(r0, CHUNK), :], s_in.astype(BF16),
                             preferred_element_type=F32) * from_start_x
            s_new = lax.dot_general(bcv_ref[pl.ds(r0, CHUNK), :], x * te_ref[slot, k],
                                    (((0,), (0,)), ((), ())), preferred_element_type=F32)
            st_ref[...] = s_in * from_start_x[edge:edge + 1, :] + s_new
            if direction == 0:
                y = y + dsk_ref[...] * x.astype(F32)
            if second_visit:
                y_ref[pl.ds(r0, CHUNK), :] = (part_ref[pl.ds(r0, CHUNK), :] + y).astype(y_ref.dtype)
            else:
                part_ref[pl.ds(r0, CHUNK), :] = y

    stf_ref[...] = jnp.zeros_like(stf_ref)
    stb_ref[...] = jnp.zeros_like(stb_ref)

    sums = running_sums(0)
    chunk_scalings(0, 0, sums)
    decay_weights(0, 0, sums)

    def two_steps(j, second_visit):
        for half in range(2):
            step = 2 * j + half
            nxt = jnp.minimum(step + 1, n_steps - 1)
            y_diag = within_chunks(step, half)
            sums = running_sums(nxt)
            across_chunks(step, half, y_diag, second_visit)
            chunk_scalings(nxt, 1 - half, sums)
            decay_weights(nxt, 1 - half, sums)

    def first_visits(j, carry):
        two_steps(j, False)
        return carry

    def second_visits(j, carry):
        two_steps(j, True)
        return carry

    lax.fori_loop(0, n_steps // 4, first_visits, 0)
    lax.fori_loop(n_steps // 4, n_steps // 2, second_visits, 0)


def _ssd_mixer(proj, dt, p, *, batch, seq):
    rows = batch * seq
    gw, ns = GROUP_WIDTH, SSD_STATE
    xs0 = MIX_WIDTH // gw
    b0 = (MIX_WIDTH + BRANCH) // ns
    c0 = b0 + SSD_GROUPS
    in_specs = [
        pl.BlockSpec((seq, gw), lambda b, g: (b, xs0 + g)),
        pl.BlockSpec((seq, ns), lambda b, g: (b, b0 + g)),
        pl.BlockSpec((seq, ns), lambda b, g: (b, c0 + g)),
        pl.BlockSpec((seq, LANES), lambda b, g: (b, g)),
        pl.BlockSpec((SSD_CONV, gw), lambda b, g: (0, g)),
        pl.BlockSpec((SSD_CONV, ns), lambda b, g: (0, BRANCH // ns + g)),
        pl.BlockSpec((SSD_CONV, ns), lambda b, g: (0, BRANCH // ns + SSD_GROUPS + g)),
        pl.BlockSpec((1, gw), lambda b, g: (0, g)),
        pl.BlockSpec((1, ns), lambda b, g: (0, BRANCH // ns + g)),
        pl.BlockSpec((1, ns), lambda b, g: (0, BRANCH // ns + SSD_GROUPS + g)),
        pl.BlockSpec((1, LANES), lambda b, g: (0, g)),
        pl.BlockSpec((1, gw), lambda b, g: (0, g)),
        pl.BlockSpec((2, 2 * LANES, gw), lambda b, g: (0, 0, 0)),
    ]
    return pl.pallas_call(
        functools.partial(_ssd_body, seq=seq),
        grid=(batch, SSD_GROUPS),
        in_specs=in_specs,
        out_specs=pl.BlockSpec((seq, gw), lambda b, g: (b, g)),
        out_shape=jax.ShapeDtypeStruct((rows, BRANCH), BF16),
        scratch_shapes=[
            pltpu.VMEM((seq + 2 * CONV_HALO, LANES), F32),
            pltpu.VMEM((seq, gw), BF16),
            pltpu.VMEM((seq, ns), BF16),
            pltpu.VMEM((seq, ns), BF16),
            pltpu.VMEM((seq, gw), F32),
            pltpu.VMEM((ns, gw), F32),
            pltpu.VMEM((ns, gw), F32),
            pltpu.VMEM((2, 2 * SSD_CHUNKS_PER_STEP, GROUP_HEADS // 2, CHUNK, 2 * CHUNK), BF16),
            pltpu.VMEM((2, 2 * SSD_CHUNKS_PER_STEP, CHUNK, gw), BF16),
            pltpu.VMEM((2, 2 * SSD_CHUNKS_PER_STEP, CHUNK, gw), F32),
        ],
        compiler_params=_params("parallel", "parallel"),
        name="ssd_mixer",
    )(proj, proj, proj, dt, p["conv_w"], p["conv_w"], p["conv_w"],
      p["conv_b"], p["conv_b"], p["conv_b"], p["a_log"], p["d_skip"], p["expand"])


def _na_body(q_ref, k_ref, v_ref, bias_ref, o_ref, *, seq):
    n_rows = seq // GRID_W
    kh = min(NA_KH, n_rows)
    win = kh * GRID_W
    assert n_rows % NA_ROWS_PER_STEP == 0
    lane = lax.broadcasted_iota(jnp.int32, (GRID_W, LANES), 1)
    lane2 = lax.broadcasted_iota(jnp.int32, (2 * GRID_W, LANES), 1)
    row2 = lax.broadcasted_iota(jnp.int32, (2 * GRID_W, LANES), 0)
    own_channels = (lane2 >= HEAD_DIM) == (row2 >= GRID_W)

    def window(r):
        rs = jnp.clip(r - kh // 2, 0, n_rows - kh)
        return pl.multiple_of(r * GRID_W, GRID_W), pl.multiple_of(rs * GRID_W, GRID_W), r - rs

    def rows_step(i, carry):
        rows = [window(i * NA_ROWS_PER_STEP + u) for u in range(NA_ROWS_PER_STEP)]
        scores = []
        for q0, k0, _ in rows:
            q = q_ref[pl.ds(q0, GRID_W), :]
            qq = jnp.concatenate([q, q], axis=0)
            qq = jnp.where(own_channels, qq, jnp.zeros_like(qq))
            scores.append(lax.dot_general(qq, k_ref[pl.ds(k0, win), :], (((1,), (1,)), ((), ())),
                                          preferred_element_type=F32))
        probs = []
        for (_, _, off), s in zip(rows, scores):
            s = s + bias_ref[off].reshape(2 * GRID_W, win)
            p = jnp.exp2(s - jnp.max(s, axis=-1, keepdims=True))
            probs.append((p.astype(BF16), jnp.sum(p, axis=-1, keepdims=True)))
        for (q0, k0, _), (p, denom) in zip(rows, probs):
            pv = jnp.dot(p, v_ref[pl.ds(k0, win), :], preferred_element_type=F32) / denom
            o = jnp.where(lane < HEAD_DIM, pv[0:GRID_W], pv[GRID_W:2 * GRID_W])
            o_ref[pl.ds(q0, GRID_W), :] = o.astype(o_ref.dtype)
        return carry

    lax.fori_loop(0, n_rows // NA_ROWS_PER_STEP, rows_step, 0)


def _na_mixer(proj, bias_tab, *, batch, seq):
    rows = batch * seq
    q0 = MIX_WIDTH // LANES
    k0 = q0 + BRANCH // LANES
    v0 = k0 + BRANCH // LANES
    n_off, _, _, win = bias_tab.shape
    return pl.pallas_call(
        functools.partial(_na_body, seq=seq),
        grid=(NA_HEADS // 2, batch),
        in_specs=[
            pl.BlockSpec((seq, LANES), lambda hp, b: (b, q0 + hp)),
            pl.BlockSpec((seq, LANES), lambda hp, b: (b, k0 + hp)),
            pl.BlockSpec((seq, LANES), lambda hp, b: (b, v0 + hp)),
            pl.BlockSpec((n_off, 2, GRID_W, win), lambda hp, b: (0, hp, 0, 0)),
        ],
        out_specs=pl.BlockSpec((seq, LANES), lambda hp, b: (b, hp)),
        out_shape=jax.ShapeDtypeStruct((rows, BRANCH), BF16),
        compiler_params=_params("parallel", "parallel"),
        name="na_mixer",
    )(proj, proj, proj, bias_tab)

def _na_bias_table(rpb, n_rows):
    kh = min(NA_KH, n_rows)
    col = jnp.arange(GRID_W)
    col_start = jnp.clip(col - NA_KW // 2, 0, GRID_W - NA_KW)
    col_in = (col[None, :] >= col_start[:, None]) & (col[None, :] < col_start[:, None] + NA_KW)
    dc = jnp.clip(col[None, :] - col[:, None], -(NA_KW - 1), NA_KW - 1) + NA_KW - 1
    by_dc = jnp.take(rpb, dc.reshape(-1), axis=2).reshape(NA_HEADS, 2 * NA_KH - 1, GRID_W, GRID_W)
    by_dc = jnp.where(col_in[None, None], by_dc * LOG2E, -jnp.inf).transpose(0, 2, 1, 3)
    flat = by_dc.reshape(NA_HEADS, GRID_W, (2 * NA_KH - 1) * GRID_W)
    tabs = [flat[:, :, (NA_KH - 1 - off) * GRID_W:(NA_KH - 1 - off + kh) * GRID_W] for off in range(kh)]
    return jnp.stack(tabs, axis=0).astype(F32)


def _out_proj_body(y_ref, z_ref, q_ref, kv_ref, x_ref, pg_ref, w_ref, o_ref, *, gated_norm):
    heads = [slice(h * MEM_HEAD_DIM, (h + 1) * MEM_HEAD_DIM) for h in range(MEM_HEADS)]
    scores = [lax.dot_general(q_ref[:, c], kv_ref[:, c], (((1,), (1,)), ((), ())),
                              preferred_element_type=F32) for c in heads]
    yg = y_ref[...].astype(F32) * _silu(z_ref[:, 0:BRANCH].astype(F32))
    acc = jnp.dot(yg.astype(BF16), w_ref[0:BRANCH, :], preferred_element_type=F32)
    if gated_norm:
        acc = acc * lax.rsqrt(jnp.mean(yg * yg, axis=-1, keepdims=True) + EPS)
    for c, s in zip(heads, scores):
        p = jnp.exp2(s - jnp.max(s, axis=-1, keepdims=True))
        denom = jnp.sum(p, axis=-1, keepdims=True)
        vcols = slice(MEM_WIDTH + c.start, MEM_WIDTH + c.stop)
        mem = jnp.dot(p.astype(BF16), kv_ref[:, vcols], preferred_element_type=F32) / denom
        gate = _silu(z_ref[:, BRANCH + c.start:BRANCH + c.stop].astype(F32))
        acc = acc + jnp.dot((mem * gate).astype(BF16), w_ref[BRANCH + c.start:BRANCH + c.stop, :],
                            preferred_element_type=F32)
    ms = jnp.mean(acc * acc, axis=-1, keepdims=True)
    o_ref[...] = x_ref[...] + acc * lax.rsqrt(ms + EPS) * pg_ref[...]


def _out_proj(y, proj, kv, x, post_g, w_out, *, seq, q_col, gated_norm):
    rows, d = x.shape
    tm = min(seq, 512)
    per_seq = seq // tm
    return pl.pallas_call(
        functools.partial(_out_proj_body, gated_norm=gated_norm),
        grid=(rows // tm,),
        in_specs=[
            pl.BlockSpec((tm, BRANCH), lambda i: (i, 0)),
            pl.BlockSpec((tm, MIX_WIDTH), lambda i: (i, 0)),
            pl.BlockSpec((tm, MEM_WIDTH), lambda i: (i, q_col)),
            pl.BlockSpec((MEM_TOKENS, 2 * MEM_WIDTH), lambda i: (i // per_seq, 0)),
            pl.BlockSpec((tm, d), lambda i: (i, 0)),
            pl.BlockSpec((1, d), lambda i: (0, 0)),
            pl.BlockSpec((MIX_WIDTH, d), lambda i: (0, 0)),
        ],
        out_specs=pl.BlockSpec((tm, d), lambda i: (i, 0)),
        out_shape=jax.ShapeDtypeStruct((rows, d), F32),
        compiler_params=_params("parallel"),
        name="out_proj",
    )(y, proj, proj, kv, x, post_g, w_out)


def _per_group_lanes(fwd, bwd):
    n = fwd.shape[0]
    f = fwd.reshape(n, SSD_GROUPS, GROUP_HEADS)
    b = bwd.reshape(n, SSD_GROUPS, GROUP_HEADS)
    pad = jnp.zeros((n, SSD_GROUPS, LANES - 2 * GROUP_HEADS), F32)
    return jnp.concatenate([f, b, pad], axis=-1).reshape(n, 1, SSD_GROUPS * LANES)


def _prepare(pre_g, post_g, mem_g, w_mem_kv, w_out, ssd_w_in, ssd_conv_w, ssd_conv_b,
             ssd_dt_bias_f, ssd_dt_bias_b, ssd_a_log_f, ssd_a_log_b, ssd_d, ssd_norm_g,
             na_w_in, na_rpb):
    xbc_end = MIX_WIDTH + BRANCH + 2 * SSD_GROUPS * SSD_STATE
    dt_end = xbc_end + 2 * SSD_HEADS
    n_ssd = ssd_w_in.shape[0]
    dt_pad = LANES - 2 * SSD_HEADS
    w_dt = jnp.pad(ssd_w_in[:, :, xbc_end:dt_end], ((0, 0), (0, 0), (0, dt_pad)))
    dt_bias = jnp.pad(jnp.concatenate([ssd_dt_bias_f, ssd_dt_bias_b], axis=-1), ((0, 0), (0, dt_pad)))
    src = lax.broadcasted_iota(jnp.int32, (2 * LANES, SSD_GROUPS * LANES), 0) % LANES
    dst = lax.broadcasted_iota(jnp.int32, (2 * LANES, SSD_GROUPS * LANES), 1)
    dst_g, dst_l = dst // LANES, dst % LANES
    dt_place = ((dst_l < 2 * GROUP_HEADS)
                & (src == SSD_HEADS * (dst_l // GROUP_HEADS) + GROUP_HEADS * dst_g + dst_l % GROUP_HEADS))
    mem_c = MEM_HEAD_DIM ** -0.5 * LOG2E
    w_main = jnp.concatenate([ssd_w_in[:, :, :xbc_end], ssd_w_in[:, :, dt_end:] * mem_c], axis=-1)
    na_cols = jnp.arange(na_w_in.shape[-1])
    na_c = jnp.where((na_cols >= MIX_WIDTH) & (na_cols < MIX_WIDTH + BRANCH), HEAD_DIM ** -0.5 * LOG2E,
                     jnp.where(na_cols >= MIX_WIDTH + 3 * BRANCH, mem_c, 1.0)).astype(F32)
    src = lax.broadcasted_iota(jnp.int32, (2, 2 * LANES, GROUP_WIDTH), 1) % LANES
    head = lax.broadcasted_iota(jnp.int32, (2, 2 * LANES, GROUP_WIDTH), 2) // HEAD_DIM
    direction = lax.broadcasted_iota(jnp.int32, (2, 2 * LANES, GROUP_WIDTH), 0)
    expand = (src == GROUP_HEADS * direction + head).astype(BF16)
    gain = jnp.ones((w_out.shape[0], MIX_WIDTH), F32).at[0::2, :BRANCH].set(ssd_norm_g)
    return dict(
        pre_g=pre_g[:, None, :], post_g=post_g[:, None, :], mem_g=mem_g[:, None, :],
        w_mem_kv=w_mem_kv.astype(BF16), w_out=(w_out * gain[:, :, None]).astype(BF16),
        ssd_w_main=w_main.astype(BF16), ssd_w_dt=w_dt.astype(BF16),
        ssd_dt_bias=dt_bias[:, None, :], ssd_dt_place=dt_place.astype(BF16),
        ssd=dict(conv_w=ssd_conv_w, conv_b=ssd_conv_b[:, None, :],
                 a_log=_per_group_lanes(ssd_a_log_f, ssd_a_log_b),
                 d_skip=jnp.repeat(ssd_d, HEAD_DIM, axis=-1)[:, None, :], expand=expand),
        na_w_in=(na_w_in * na_c).astype(BF16), na_rpb=na_rpb,
    )


def _encoder(x, mem, p):
    batch, seq, d = x.shape
    xf = x.reshape(batch * seq, d)
    memf = mem.reshape(batch * MEM_TOKENS, d)
    bias_tabs = [_na_bias_table(p["na_rpb"][j], seq // GRID_W) for j in range(p["na_rpb"].shape[0])]
    for i in range(DEPTH):
        j = i // 2
        kv = _norm_matmul(memf, p["mem_g"][i], p["w_mem_kv"][i], name="mem_kv")
        if i % 2 == 0:
            dt_params = (p["ssd_w_dt"][j], p["ssd_dt_bias"][j], p["ssd_dt_place"])
            proj, dt = _norm_matmul(xf, p["pre_g"][i], p["ssd_w_main"][j], dt_params, name="ssd_in_proj")
            layer = {k: v[j] for k, v in p["ssd"].items() if k != "expand"}
            layer["expand"] = p["ssd"]["expand"]
            y = _ssd_mixer(proj, dt, layer, batch=batch, seq=seq)
            q_col = (MIX_WIDTH + BRANCH + 2 * SSD_GROUPS * SSD_STATE) // MEM_WIDTH
        else:
            proj = _norm_matmul(xf, p["pre_g"][i], p["na_w_in"][j], name="na_in_proj")
            y = _na_mixer(proj, bias_tabs[j], batch=batch, seq=seq)
            q_col = (MIX_WIDTH + 3 * BRANCH) // MEM_WIDTH
        xf = _out_proj(y, proj, kv, xf, p["post_g"][i], p["w_out"][i], seq=seq, q_col=q_col,
                       gated_norm=(i % 2 == 0))
    return xf.reshape(batch, seq, d)


def kernel(x_prompt, x_sample, mem_prompt, mem_sample, pre_g, post_g, mem_g, w_mem_kv, w_out, ssd_w_in, ssd_conv_w, ssd_conv_b, ssd_dt_bias_f, ssd_dt_bias_b, ssd_a_log_f, ssd_a_log_b, ssd_d, ssd_norm_g, na_w_in, na_rpb):
    p = _prepare(pre_g, post_g, mem_g, w_mem_kv, w_out, ssd_w_in, ssd_conv_w, ssd_conv_b,
                 ssd_dt_bias_f, ssd_dt_bias_b, ssd_a_log_f, ssd_a_log_b, ssd_d, ssd_norm_g,
                 na_w_in, na_rpb)
    return _encoder(x_prompt, mem_prompt, p), _encoder(x_sample, mem_sample, p)
```

```python
import functools

import jax
import jax.numpy as jnp
from jax import lax
from jax.experimental import pallas as pl
from jax.experimental.pallas import tpu as pltpu

F32 = jnp.float32
BF16 = jnp.bfloat16

DEPTH = 4
D_MODEL = 1024
GRID_W = 64
BRANCH = 2048
HEAD_DIM = 64
SSD_HEADS = 32
SSD_GROUPS = 4
GROUP_HEADS = SSD_HEADS // SSD_GROUPS
GROUP_WIDTH = GROUP_HEADS * HEAD_DIM
SSD_STATE = 128
SSD_CONV = 5
CHUNK = 128
NA_HEADS = 32
NA_KH = 8
NA_KW = 16
MEM_TOKENS = 256
MEM_HEADS = 4
MEM_HEAD_DIM = 256
MEM_WIDTH = MEM_HEADS * MEM_HEAD_DIM
MIX_WIDTH = BRANCH + MEM_WIDTH
EPS = 1e-6
LOG2E = 1.4426950408889634

LANES = 128
MXU_COLS = 256
PROJ_MAX_TN = 2048
PROJ_TM = 2048
SSD_CHUNKS_PER_STEP = 2
NA_ROWS_PER_STEP = 32
CONV_TILE = 512
CONV_HALO = 8
VMEM_LIMIT = 56 * 1024 * 1024


def _params(*semantics):
    return pltpu.CompilerParams(dimension_semantics=semantics, vmem_limit_bytes=VMEM_LIMIT)


def _silu(v):
    return v * jax.nn.sigmoid(v)


def _softplus(v):
    return jnp.maximum(v, 0.0) + jnp.log1p(jnp.exp(-jnp.abs(v)))


def _split_bf16(v):
    hi = v.astype(BF16)
    lo = (v - hi.astype(F32)).astype(BF16)
    return jnp.concatenate([hi, lo], axis=1)


def _norm_matmul_body(x_ref, g_ref, w_ref, *rest, has_dt):
    if has_dt:
        wdt_ref, dtb_ref, place_ref, o_ref, dt_ref, h_ref = rest
    else:
        o_ref, h_ref = rest

    @pl.when(pl.program_id(1) == 0)
    def _():
        x = x_ref[...]
        ms = jnp.mean(x * x, axis=-1, keepdims=True)
        h = (x * lax.rsqrt(ms + EPS) * g_ref[...]).astype(BF16)
        h_ref[...] = h
        if has_dt:
            dt = _softplus(jnp.dot(h, wdt_ref[...], preferred_element_type=F32) + dtb_ref[...])
            dt_ref[...] = jnp.dot(_split_bf16(dt), place_ref[...], preferred_element_type=F32)

    o_ref[...] = jnp.dot(h_ref[...], w_ref[...], preferred_element_type=F32).astype(o_ref.dtype)


def _norm_matmul(x, g, w, dt_params=None, *, name):
    m, d = x.shape
    wn = w.shape[1]
    tm = min(m, PROJ_TM if dt_params is None else PROJ_TM // 2)
    tn = max(t for t in range(MXU_COLS, PROJ_MAX_TN + 1, MXU_COLS) if wn % t == 0)
    in_specs = [
        pl.BlockSpec((tm, d), lambda i, j: (i, 0)),
        pl.BlockSpec((1, d), lambda i, j: (0, 0)),
        pl.BlockSpec((d, tn), lambda i, j: (0, j)),
    ]
    out_shape = [jax.ShapeDtypeStruct((m, wn), BF16)]
    out_specs = [pl.BlockSpec((tm, tn), lambda i, j: (i, j))]
    args = [x, g, w]
    if dt_params is not None:
        for a in dt_params:
            in_specs.append(pl.BlockSpec(a.shape, lambda i, j: (0, 0)))
        dn = dt_params[2].shape[1]
        out_shape.append(jax.ShapeDtypeStruct((m, dn), F32))
        out_specs.append(pl.BlockSpec((tm, dn), lambda i, j: (i, 0)))
        args += list(dt_params)
    outs = pl.pallas_call(
        functools.partial(_norm_matmul_body, has_dt=dt_params is not None),
        grid=(m // tm, wn // tn),
        in_specs=in_specs,
        out_specs=out_specs,
        out_shape=out_shape,
        scratch_shapes=[pltpu.VMEM((tm, d), BF16)],
        compiler_params=_params("parallel", "arbitrary"),
        name=name,
    )(*args)
    return outs if dt_params is not None else outs[0]


def _conv_silu(src_ref, w_ref, b_ref, dst_ref, pad_ref, *, seq, width):
    n_tiles = seq // CONV_TILE
    zeros = jnp.zeros((CONV_HALO, LANES), F32)
    for cc in range(width // LANES):
        cols = slice(cc * LANES, (cc + 1) * LANES)
        pad_ref[0:CONV_HALO, :] = zeros
        pad_ref[seq + CONV_HALO:seq + 2 * CONV_HALO, :] = zeros

        def fill(i, carry, cols=cols):
            r0 = pl.multiple_of(i * CONV_TILE, CONV_TILE)
            dst = pl.multiple_of(r0 + CONV_HALO, CONV_HALO)
            pad_ref[pl.ds(dst, CONV_TILE), :] = src_ref[pl.ds(r0, CONV_TILE), cols].astype(F32)
            return carry

        lax.fori_loop(0, n_tiles, fill, 0)
        w = w_ref[:, cols]
        bias = b_ref[:, cols]

        def tile(i, carry, cols=cols, w=w, bias=bias):
            r0 = pl.multiple_of(i * CONV_TILE, CONV_TILE)
            acc = jnp.broadcast_to(bias, (CONV_TILE, LANES))
            for k in range(SSD_CONV):
                off = CONV_HALO - SSD_CONV // 2 + k
                acc = acc + w[k:k + 1, :] * pad_ref[pl.ds(r0 + off, CONV_TILE), :]
            dst_ref[pl.ds(r0, CONV_TILE), cols] = _silu(acc).astype(BF16)
            return carry

        lax.fori_loop(0, n_tiles, tile, 0)


def _ssd_body(xs_ref, bp_ref, cp_ref, dt_ref, wx_ref, wb_ref, wc_ref, bx_ref, bb_ref, bc_ref,
              alog_ref, dsk_ref, e_ref, y_ref,
              pad_ref, xc_ref, bcv_ref, ccv_ref, part_ref, stf_ref, stb_ref, w_ref, te_ref, fs_ref,
              *, seq):
    n_chunks = seq // CHUNK
    n_steps = n_chunks // SSD_CHUNKS_PER_STEP
    assert n_chunks % (4 * SSD_CHUNKS_PER_STEP) == 0
    _conv_silu(xs_ref, wx_ref, bx_ref, xc_ref, pad_ref, seq=seq, width=GROUP_WIDTH)
    _conv_silu(bp_ref, wb_ref, bb_ref, bcv_ref, pad_ref, seq=seq, width=SSD_STATE)
    _conv_silu(cp_ref, wc_ref, bc_ref, ccv_ref, pad_ref, seq=seq, width=SSD_STATE)

    row = lax.broadcasted_iota(jnp.int32, (CHUNK, CHUNK), 0)
    col = lax.broadcasted_iota(jnp.int32, (CHUNK, CHUNK), 1)
    lane = lax.broadcasted_iota(jnp.int32, (CHUNK, LANES), 1)
    neg_a = -jnp.exp(alog_ref[...])

    def chains_of(step):
        chains = []
        for u in range(SSD_CHUNKS_PER_STEP):
            c = step * SSD_CHUNKS_PER_STEP + u
            chains += [(c, 0, stf_ref), (n_chunks - 1 - c, 1, stb_ref)]
        return [(pl.multiple_of(c * CHUNK, CHUNK), d, st) for c, d, st in chains]

    def keep_mask(direction):
        return (col <= row) if direction == 0 else (col >= row)

    def running_sums(step):
        out = []
        for r0, direction, _ in chains_of(step):
            dt = dt_ref[pl.ds(r0, CHUNK), :]
            tri = jnp.where(keep_mask(direction), 1.0, 0.0).astype(BF16)
            acs2 = jnp.dot(tri, _split_bf16(neg_a * dt), preferred_element_type=F32)
            out.append((dt, acs2[:, :LANES] + acs2[:, LANES:]))
        return out

    def chunk_scalings(step, slot, sums):
        for k, ((r0, direction, _), (dt, acs)) in enumerate(zip(chains_of(step), sums)):
            edge = CHUNK - 1 if direction == 0 else 0
            to_end_dt = (jnp.exp(acs[edge:edge + 1, :] - acs) * dt).astype(BF16)
            te_ref[slot, k] = jnp.dot(to_end_dt, e_ref[direction, 0:LANES, :],
                                      preferred_element_type=F32).astype(BF16)
            fs_ref[slot, k] = jnp.dot(_split_bf16(jnp.exp(acs)), e_ref[direction],
                                      preferred_element_type=F32)

    def decay_weights(step, slot, sums):
        for k, ((r0, direction, _), (dt, acs)) in enumerate(zip(chains_of(step), sums)):
            cb = lax.dot_general(ccv_ref[pl.ds(r0, CHUNK), :], bcv_ref[pl.ds(r0, CHUNK), :],
                                 (((1,), (1,)), ((), ())), preferred_element_type=F32)
            query = acs * LOG2E
            source_t = (query - jnp.log2(dt)).T
            keep = keep_mask(direction)
            for pair in range(GROUP_HEADS // 2):
                w_pair = []
                for hh in range(2):
                    j = GROUP_HEADS * direction + 2 * pair + hh
                    seg = query[:, j:j + 1] - source_t[j:j + 1, :]
                    w_pair.append((jnp.exp2(jnp.where(keep, seg, -jnp.inf)) * cb).astype(BF16))
                w_ref[slot, k, pair] = jnp.concatenate(w_pair, axis=1)

    def within_chunks(step, slot):
        out = []
        for k, (r0, _, _) in enumerate(chains_of(step)):
            x = xc_ref[pl.ds(r0, CHUNK), :]
            y_parts = []
            for pair in range(GROUP_HEADS // 2):
                xp = x[:, pair * LANES:(pair + 1) * LANES]
                zero = jnp.zeros_like(xp)
                x_bd = jnp.concatenate([jnp.where(lane < HEAD_DIM, xp, zero),
                                        jnp.where(lane >= HEAD_DIM, xp, zero)], axis=0)
                y_parts.append(jnp.dot(w_ref[slot, k, pair], x_bd, preferred_element_type=F32))
            out.append(jnp.concatenate(y_parts, axis=1))
        return out

    def across_chunks(step, slot, y_diag, second_visit):
        for k, ((r0, direction, st_ref), yd) in enumerate(zip(chains_of(step), y_diag)):
            edge = CHUNK - 1 if direction == 0 else 0
            x = xc_ref[pl.ds(r0, CHUNK), :]
            from_start_x = fs_ref[slot, k]
            s_in = st_ref[...]
            y = yd + jnp.dot(ccv_ref[pl.ds(r0, CHUNK), :], s_in.astype(BF16),
                             preferred_element_type=F32) * from_start_x
            s_new = lax.dot_general(bcv_ref[pl.ds(r0, CHUNK), :], x * te_ref[slot, k],
                                    (((0,), (0,)), ((), ())), preferred_element_type=F32)
            st_ref[...] = s_in * from_start_x[edge:edge + 1, :] + s_new
            if direction == 0:
                y = y + dsk_ref[...] * x.astype(F32)
            if second_visit:
                y_ref[pl.ds(r0, CHUNK), :] = (part_ref[pl.ds(r0, CHUNK), :] + y).astype(y_ref.dtype)
            else:
                part_ref[pl.ds(r0, CHUNK), :] = y

    stf_ref[...] = jnp.zeros_like(stf_ref)
    stb_ref[...] = jnp.zeros_like(stb_ref)

    sums = running_sums(0)
    chunk_scalings(0, 0, sums)
    decay_weights(0, 0, sums)

    def two_steps(j, second_visit):
        for half in range(2):
            step = 2 * j + half
            nxt = jnp.minimum(step + 1, n_steps - 1)
            y_diag = within_chunks(step, half)
            sums = running_sums(nxt)
            across_chunks(step, half, y_diag, second_visit)
            chunk_scalings(nxt, 1 - half, sums)
            decay_weights(nxt, 1 - half, sums)

    def first_visits(j, carry):
        two_steps(j, False)
        return carry

    def second_visits(j, carry):
        two_steps(j, True)
        return carry

    lax.fori_loop(0, n_steps // 4, first_visits, 0)
    lax.fori_loop(n_steps // 4, n_steps // 2, second_visits, 0)


def _ssd_mixer(proj, dt, p, *, batch, seq):
    rows = batch * seq
    gw, ns = GROUP_WIDTH, SSD_STATE
    xs0 = MIX_WIDTH // gw
    b0 = (MIX_WIDTH + BRANCH) // ns
    c0 = b0 + SSD_GROUPS
    in_specs = [
        pl.BlockSpec((seq, gw), lambda b, g: (b, xs0 + g)),
        pl.BlockSpec((seq, ns), lambda b, g: (b, b0 + g)),
        pl.BlockSpec((seq, ns), lambda b, g: (b, c0 + g)),
        pl.BlockSpec((seq, LANES), lambda b, g: (b, g)),
        pl.BlockSpec((SSD_CONV, gw), lambda b, g: (0, g)),
        pl.BlockSpec((SSD_CONV, ns), lambda b, g: (0, BRANCH // ns + g)),
        pl.BlockSpec((SSD_CONV, ns), lambda b, g: (0, BRANCH // ns + SSD_GROUPS + g)),
        pl.BlockSpec((1, gw), lambda b, g: (0, g)),
        pl.BlockSpec((1, ns), lambda b, g: (0, BRANCH // ns + g)),
        pl.BlockSpec((1, ns), lambda b, g: (0, BRANCH // ns + SSD_GROUPS + g)),
        pl.BlockSpec((1, LANES), lambda b, g: (0, g)),
        pl.BlockSpec((1, gw), lambda b, g: (0, g)),
        pl.BlockSpec((2, 2 * LANES, gw), lambda b, g: (0, 0, 0)),
    ]
    return pl.pallas_call(
        functools.partial(_ssd_body, seq=seq),
        grid=(batch, SSD_GROUPS),
        in_specs=in_specs,
        out_specs=pl.BlockSpec((seq, gw), lambda b, g: (b, g)),
        out_shape=jax.ShapeDtypeStruct((rows, BRANCH), BF16),
        scratch_shapes=[
            pltpu.VMEM((seq + 2 * CONV_HALO, LANES), F32),
            pltpu.VMEM((seq, gw), BF16),
            pltpu.VMEM((seq, ns), BF16),
            pltpu.VMEM((seq, ns), BF16),
            pltpu.VMEM((seq, gw), F32),
            pltpu.VMEM((ns, gw), F32),
            pltpu.VMEM((ns, gw), F32),
            pltpu.VMEM((2, 2 * SSD_CHUNKS_PER_STEP, GROUP_HEADS // 2, CHUNK, 2 * CHUNK), BF16),
            pltpu.VMEM((2, 2 * SSD_CHUNKS_PER_STEP, CHUNK, gw), BF16),
            pltpu.VMEM((2, 2 * SSD_CHUNKS_PER_STEP, CHUNK, gw), F32),
        ],
        compiler_params=_params("parallel", "parallel"),
        name="ssd_mixer",
    )(proj, proj, proj, dt, p["conv_w"], p["conv_w"], p["conv_w"],
      p["conv_b"], p["conv_b"], p["conv_b"], p["a_log"], p["d_skip"], p["expand"])


def _na_body(q_ref, k_ref, v_ref, bias_ref, o_ref, *, seq):
    n_rows = seq // GRID_W
    kh = min(NA_KH, n_rows)
    win = kh * GRID_W
    assert n_rows % NA_ROWS_PER_STEP == 0
    lane = lax.broadcasted_iota(jnp.int32, (GRID_W, LANES), 1)
    lane2 = lax.broadcasted_iota(jnp.int32, (2 * GRID_W, LANES), 1)
    row2 = lax.broadcasted_iota(jnp.int32, (2 * GRID_W, LANES), 0)
    own_channels = (lane2 >= HEAD_DIM) == (row2 >= GRID_W)

    def window(r):
        rs = jnp.clip(r - kh // 2, 0, n_rows - kh)
        return pl.multiple_of(r * GRID_W, GRID_W), pl.multiple_of(rs * GRID_W, GRID_W), r - rs

    def rows_step(i, carry):
        rows = [window(i * NA_ROWS_PER_STEP + u) for u in range(NA_ROWS_PER_STEP)]
        scores = []
        for q0, k0, _ in rows:
            q = q_ref[pl.ds(q0, GRID_W), :]
            qq = jnp.concatenate([q, q], axis=0)
            qq = jnp.where(own_channels, qq, jnp.zeros_like(qq))
            scores.append(lax.dot_general(qq, k_ref[pl.ds(k0, win), :], (((1,), (1,)), ((), ())),
                                          preferred_element_type=F32))
        probs = []
        for (_, _, off), s in zip(rows, scores):
            s = s + bias_ref[off].reshape(2 * GRID_W, win)
            p = jnp.exp2(s - jnp.max(s, axis=-1, keepdims=True))
            probs.append((p.astype(BF16), jnp.sum(p, axis=-1, keepdims=True)))
        for (q0, k0, _), (p, denom) in zip(rows, probs):
            pv = jnp.dot(p, v_ref[pl.ds(k0, win), :], preferred_element_type=F32) / denom
            o = jnp.where(lane < HEAD_DIM, pv[0:GRID_W], pv[GRID_W:2 * GRID_W])
            o_ref[pl.ds(q0, GRID_W), :] = o.astype(o_ref.dtype)
        return carry

    lax.fori_loop(0, n_rows // NA_ROWS_PER_STEP, rows_step, 0)


def _na_mixer(proj, bias_tab, *, batch, seq):
    rows = batch * seq
    q0 = MIX_WIDTH // LANES
    k0 = q0 + BRANCH // LANES
    v0 = k0 + BRANCH // LANES
    n_off, _, _, win = bias_tab.shape
    return pl.pallas_call(
        functools.partial(_na_body, seq=seq),
        grid=(NA_HEADS // 2, batch),
        in_specs=[
            pl.BlockSpec((seq, LANES), lambda hp, b: (b, q0 + hp)),
            pl.BlockSpec((seq, LANES), lambda hp, b: (b, k0 + hp)),
            pl.BlockSpec((seq, LANES), lambda hp, b: (b, v0 + hp)),
            pl.BlockSpec((n_off, 2, GRID_W, win), lambda hp, b: (0, hp, 0, 0)),
        ],
        out_specs=pl.BlockSpec((seq, LANES), lambda hp, b: (b, hp)),
        out_shape=jax.ShapeDtypeStruct((rows, BRANCH), BF16),
        compiler_params=_params("parallel", "parallel"),
        name="na_mixer",
    )(proj, proj, proj, bias_tab)

def _na_bias_table(rpb, n_rows):
    kh = min(NA_KH, n_rows)
    col = jnp.arange(GRID_W)
    col_start = jnp.clip(col - NA_KW // 2, 0, GRID_W - NA_KW)
    col_in = (col[None, :] >= col_start[:, None]) & (col[None, :] < col_start[:, None] + NA_KW)
    dc = jnp.clip(col[None, :] - col[:, None], -(NA_KW - 1), NA_KW - 1) + NA_KW - 1
    by_dc = jnp.take(rpb, dc.reshape(-1), axis=2).reshape(NA_HEADS, 2 * NA_KH - 1, GRID_W, GRID_W)
    by_dc = jnp.where(col_in[None, None], by_dc * LOG2E, -jnp.inf).transpose(0, 2, 1, 3)
    flat = by_dc.reshape(NA_HEADS, GRID_W, (2 * NA_KH - 1) * GRID_W)
    tabs = [flat[:, :, (NA_KH - 1 - off) * GRID_W:(NA_KH - 1 - off + kh) * GRID_W] for off in range(kh)]
    return jnp.stack(tabs, axis=0).astype(F32)


def _out_proj_body(y_ref, z_ref, q_ref, kv_ref, x_ref, pg_ref, w_ref, o_ref, *, gated_norm):
    heads = [slice(h * MEM_HEAD_DIM, (h + 1) * MEM_HEAD_DIM) for h in range(MEM_HEADS)]
    scores = [lax.dot_general(q_ref[:, c], kv_ref[:, c], (((1,), (1,)), ((), ())),
                              preferred_element_type=F32) for c in heads]
    yg = y_ref[...].astype(F32) * _silu(z_ref[:, 0:BRANCH].astype(F32))
    acc = jnp.dot(yg.astype(BF16), w_ref[0:BRANCH, :], preferred_element_type=F32)
    if gated_norm:
        acc = acc * lax.rsqrt(jnp.mean(yg * yg, axis=-1, keepdims=True) + EPS)
    for c, s in zip(heads, scores):
        p = jnp.exp2(s - jnp.max(s, axis=-1, keepdims=True))
        denom = jnp.sum(p, axis=-1, keepdims=True)
        vcols = slice(MEM_WIDTH + c.start, MEM_WIDTH + c.stop)
        mem = jnp.dot(p.astype(BF16), kv_ref[:, vcols], preferred_element_type=F32) / denom
        gate = _silu(z_ref[:, BRANCH + c.start:BRANCH + c.stop].astype(F32))
        acc = acc + jnp.dot((mem * gate).astype(BF16), w_ref[BRANCH + c.start:BRANCH + c.stop, :],
                            preferred_element_type=F32)
    ms = jnp.mean(acc * acc, axis=-1, keepdims=True)
    o_ref[...] = x_ref[...] + acc * lax.rsqrt(ms + EPS) * pg_ref[...]


def _out_proj(y, proj, kv, x, post_g, w_out, *, seq, q_col, gated_norm):
    rows, d = x.shape
    tm = min(seq, 512)
    per_seq = seq // tm
    return pl.pallas_call(
        functools.partial(_out_proj_body, gated_norm=gated_norm),
        grid=(rows // tm,),
        in_specs=[
            pl.BlockSpec((tm, BRANCH), lambda i: (i, 0)),
            pl.BlockSpec((tm, MIX_WIDTH), lambda i: (i, 0)),
            pl.BlockSpec((tm, MEM_WIDTH), lambda i: (i, q_col)),
            pl.BlockSpec((MEM_TOKENS, 2 * MEM_WIDTH), lambda i: (i // per_seq, 0)),
            pl.BlockSpec((tm, d), lambda i: (i, 0)),
            pl.BlockSpec((1, d), lambda i: (0, 0)),
            pl.BlockSpec((MIX_WIDTH, d), lambda i: (0, 0)),
        ],
        out_specs=pl.BlockSpec((tm, d), lambda i: (i, 0)),
        out_shape=jax.ShapeDtypeStruct((rows, d), F32),
        compiler_params=_params("parallel"),
        name="out_proj",
    )(y, proj, proj, kv, x, post_g, w_out)


def _per_group_lanes(fwd, bwd):
    n = fwd.shape[0]
    f = fwd.reshape(n, SSD_GROUPS, GROUP_HEADS)
    b = bwd.reshape(n, SSD_GROUPS, GROUP_HEADS)
    pad = jnp.zeros((n, SSD_GROUPS, LANES - 2 * GROUP_HEADS), F32)
    return jnp.concatenate([f, b, pad], axis=-1).reshape(n, 1, SSD_GROUPS * LANES)


def _prepare(pre_g, post_g, mem_g, w_mem_kv, w_out, ssd_w_in, ssd_conv_w, ssd_conv_b,
             ssd_dt_bias_f, ssd_dt_bias_b, ssd_a_log_f, ssd_a_log_b, ssd_d, ssd_norm_g,
             na_w_in, na_rpb):
    xbc_end = MIX_WIDTH + BRANCH + 2 * SSD_GROUPS * SSD_STATE
    dt_end = xbc_end + 2 * SSD_HEADS
    n_ssd = ssd_w_in.shape[0]
    dt_pad = LANES - 2 * SSD_HEADS
    w_dt = jnp.pad(ssd_w_in[:, :, xbc_end:dt_end], ((0, 0), (0, 0), (0, dt_pad)))
    dt_bias = jnp.pad(jnp.concatenate([ssd_dt_bias_f, ssd_dt_bias_b], axis=-1), ((0, 0), (0, dt_pad)))
    src = lax.broadcasted_iota(jnp.int32, (2 * LANES, SSD_GROUPS * LANES), 0) % LANES
    dst = lax.broadcasted_iota(jnp.int32, (2 * LANES, SSD_GROUPS * LANES), 1)
    dst_g, dst_l = dst // LANES, dst % LANES
    dt_place = ((dst_l < 2 * GROUP_HEADS)
                & (src == SSD_HEADS * (dst_l // GROUP_HEADS) + GROUP_HEADS * dst_g + dst_l % GROUP_HEADS))
    mem_c = MEM_HEAD_DIM ** -0.5 * LOG2E
    w_main = jnp.concatenate([ssd_w_in[:, :, :xbc_end], ssd_w_in[:, :, dt_end:] * mem_c], axis=-1)
    na_cols = jnp.arange(na_w_in.shape[-1])
    na_c = jnp.where((na_cols >= MIX_WIDTH) & (na_cols < MIX_WIDTH + BRANCH), HEAD_DIM ** -0.5 * LOG2E,
                     jnp.where(na_cols >= MIX_WIDTH + 3 * BRANCH, mem_c, 1.0)).astype(F32)
    src = lax.broadcasted_iota(jnp.int32, (2, 2 * LANES, GROUP_WIDTH), 1) % LANES
    head = lax.broadcasted_iota(jnp.int32, (2, 2 * LANES, GROUP_WIDTH), 2) // HEAD_DIM
    direction = lax.broadcasted_iota(jnp.int32, (2, 2 * LANES, GROUP_WIDTH), 0)
    expand = (src == GROUP_HEADS * direction + head).astype(BF16)
    gain = jnp.ones((w_out.shape[0], MIX_WIDTH), F32).at[0::2, :BRANCH].set(ssd_norm_g)
    return dict(
        pre_g=pre_g[:, None, :], post_g=post_g[:, None, :], mem_g=mem_g[:, None, :],
        w_mem_kv=w_mem_kv.astype(BF16), w_out=(w_out * gain[:, :, None]).astype(BF16),
        ssd_w_main=w_main.astype(BF16), ssd_w_dt=w_dt.astype(BF16),
        ssd_dt_bias=dt_bias[:, None, :], ssd_dt_place=dt_place.astype(BF16),
        ssd=dict(conv_w=ssd_conv_w, conv_b=ssd_conv_b[:, None, :],
                 a_log=_per_group_lanes(ssd_a_log_f, ssd_a_log_b),
                 d_skip=jnp.repeat(ssd_d, HEAD_DIM, axis=-1)[:, None, :], expand=expand),
        na_w_in=(na_w_in * na_c).astype(BF16), na_rpb=na_rpb,
    )


def _encoder(x, mem, p):
    batch, seq, d = x.shape
    xf = x.reshape(batch * seq, d)
    memf = mem.reshape(batch * MEM_TOKENS, d)
    bias_tabs = [_na_bias_table(p["na_rpb"][j], seq // GRID_W) for j in range(p["na_rpb"].shape[0])]
    for i in range(DEPTH):
        j = i // 2
        kv = _norm_matmul(memf, p["mem_g"][i], p["w_mem_kv"][i], name="mem_kv")
        if i % 2 == 0:
            dt_params = (p["ssd_w_dt"][j], p["ssd_dt_bias"][j], p["ssd_dt_place"])
            proj, dt = _norm_matmul(xf, p["pre_g"][i], p["ssd_w_main"][j], dt_params, name="ssd_in_proj")
            layer = {k: v[j] for k, v in p["ssd"].items() if k != "expand"}
            layer["expand"] = p["ssd"]["expand"]
            y = _ssd_mixer(proj, dt, layer, batch=batch, seq=seq)
            q_col = (MIX_WIDTH + BRANCH + 2 * SSD_GROUPS * SSD_STATE) // MEM_WIDTH
        else:
            proj = _norm_matmul(xf, p["pre_g"][i], p["na_w_in"][j], name="na_in_proj")
            y = _na_mixer(proj, bias_tabs[j], batch=batch, seq=seq)
            q_col = (MIX_WIDTH + 3 * BRANCH) // MEM_WIDTH
        xf = _out_proj(y, proj, kv, xf, p["post_g"][i], p["w_out"][i], seq=seq, q_col=q_col,
                       gated_norm=(i % 2 == 0))
    return xf.reshape(batch, seq, d)


def kernel(x_prompt, x_sample, mem_prompt, mem_sample, pre_g, post_g, mem_g, w_mem_kv, w_out, ssd_w_in, ssd_conv_w, ssd_conv_b, ssd_dt_bias_f, ssd_dt_bias_b, ssd_a_log_f, ssd_a_log_b, ssd_d, ssd_norm_g, na_w_in, na_rpb):
    p = _prepare(pre_g, post_g, mem_g, w_mem_kv, w_out, ssd_w_in, ssd_conv_w, ssd_conv_b,
                 ssd_dt_bias_f, ssd_dt_bias_b, ssd_a_log_f, ssd_a_log_b, ssd_d, ssd_norm_g,
                 na_w_in, na_rpb)
    return _encoder(x_prompt, mem_prompt, p), _encoder(x_sample, mem_sample, p)
```

```python
import functools

import jax
import jax.numpy as jnp
from jax import lax
from jax.experimental import pallas as pl
from jax.experimental.pallas import tpu as pltpu

F32 = jnp.float32
BF16 = jnp.bfloat16

DEPTH = 4
D_MODEL = 1024
GRID_W = 64
BRANCH = 2048
HEAD_DIM = 64
SSD_HEADS = 32
SSD_GROUPS = 4
GROUP_HEADS = SSD_HEADS // SSD_GROUPS
GROUP_WIDTH = GROUP_HEADS * HEAD_DIM
SSD_STATE = 128
SSD_CONV = 5
CHUNK = 128
NA_HEADS = 32
NA_KH = 8
NA_KW = 16
MEM_TOKENS = 256
MEM_HEADS = 4
MEM_HEAD_DIM = 256
MEM_WIDTH = MEM_HEADS * MEM_HEAD_DIM
MIX_WIDTH = BRANCH + MEM_WIDTH
EPS = 1e-6
LOG2E = 1.4426950408889634

LANES = 128
MXU_COLS = 256
PROJ_MAX_TN = 2048
PROJ_TM = 2048
SSD_CHUNKS_PER_STEP = 2
NA_ROWS_PER_STEP = 64
CONV_TILE = 1024
CONV_HALO = 8
VMEM_LIMIT = 56 * 1024 * 1024


def _params(*semantics):
    return pltpu.CompilerParams(dimension_semantics=semantics, vmem_limit_bytes=VMEM_LIMIT)


def _silu(v):
    return v * jax.nn.sigmoid(v)


def _softplus(v):
    return jnp.maximum(v, 0.0) + jnp.log1p(jnp.exp(-jnp.abs(v)))


def _split_bf16(v):
    hi = v.astype(BF16)
    lo = (v - hi.astype(F32)).astype(BF16)
    return jnp.concatenate([hi, lo], axis=1)


def _norm_matmul_body(x_ref, g_ref, w_ref, *rest, has_dt):
    if has_dt:
        wdt_ref, dtb_ref, place_ref, o_ref, dt_ref, h_ref = rest
    else:
        o_ref, h_ref = rest

    @pl.when(pl.program_id(1) == 0)
    def _():
        x = x_ref[...]
        ms = jnp.mean(x * x, axis=-1, keepdims=True)
        h = (x * lax.rsqrt(ms + EPS) * g_ref[...]).astype(BF16)
        h_ref[...] = h
        if has_dt:
            dt = _softplus(jnp.dot(h, wdt_ref[...], preferred_element_type=F32) + dtb_ref[...])
            dt_ref[...] = jnp.dot(_split_bf16(dt), place_ref[...], preferred_element_type=F32)

    o_ref[...] = jnp.dot(h_ref[...], w_ref[...], preferred_element_type=F32).astype(o_ref.dtype)


def _norm_matmul(x, g, w, dt_params=None, *, name):
    m, d = x.shape
    wn = w.shape[1]
    tm = min(m, PROJ_TM if dt_params is None else PROJ_TM // 2)
    tn = max(t for t in range(MXU_COLS, PROJ_MAX_TN + 1, MXU_COLS) if wn % t == 0)
    in_specs = [
        pl.BlockSpec((tm, d), lambda i, j: (i, 0)),
        pl.BlockSpec((1, d), lambda i, j: (0, 0)),
        pl.BlockSpec((d, tn), lambda i, j: (0, j)),
    ]
    out_shape = [jax.ShapeDtypeStruct((m, wn), BF16)]
    out_specs = [pl.BlockSpec((tm, tn), lambda i, j: (i, j))]
    args = [x, g, w]
    if dt_params is not None:
        for a in dt_params:
            in_specs.append(pl.BlockSpec(a.shape, lambda i, j: (0, 0)))
        dn = dt_params[2].shape[1]
        out_shape.append(jax.ShapeDtypeStruct((m, dn), F32))
        out_specs.append(pl.BlockSpec((tm, dn), lambda i, j: (i, 0)))
        args += list(dt_params)
    outs = pl.pallas_call(
        functools.partial(_norm_matmul_body, has_dt=dt_params is not None),
        grid=(m // tm, wn // tn),
        in_specs=in_specs,
        out_specs=out_specs,
        out_shape=out_shape,
        scratch_shapes=[pltpu.VMEM((tm, d), BF16)],
        compiler_params=_params("parallel", "arbitrary"),
        name=name,
    )(*args)
    return outs if dt_params is not None else outs[0]


def _conv_silu(src_ref, w_ref, b_ref, dst_ref, pad_ref, *, seq, width):
    n_tiles = seq // CONV_TILE
    zeros = jnp.zeros((CONV_HALO, LANES), F32)
    for cc in range(width // LANES):
        cols = slice(cc * LANES, (cc + 1) * LANES)
        pad_ref[0:CONV_HALO, :] = zeros
        pad_ref[seq + CONV_HALO:seq + 2 * CONV_HALO, :] = zeros

        def fill(i, carry, cols=cols):
            r0 = pl.multiple_of(i * CONV_TILE, CONV_TILE)
            dst = pl.multiple_of(r0 + CONV_HALO, CONV_HALO)
            pad_ref[pl.ds(dst, CONV_TILE), :] = src_ref[pl.ds(r0, CONV_TILE), cols].astype(F32)
            return carry

        lax.fori_loop(0, n_tiles, fill, 0)
        w = w_ref[:, cols]
        bias = b_ref[:, cols]

        def tile(i, carry, cols=cols, w=w, bias=bias):
            r0 = pl.multiple_of(i * CONV_TILE, CONV_TILE)
            acc = jnp.broadcast_to(bias, (CONV_TILE, LANES))
            for k in range(SSD_CONV):
                off = CONV_HALO - SSD_CONV // 2 + k
                acc = acc + w[k:k + 1, :] * pad_ref[pl.ds(r0 + off, CONV_TILE), :]
            dst_ref[pl.ds(r0, CONV_TILE), cols] = _silu(acc).astype(BF16)
            return carry

        lax.fori_loop(0, n_tiles, tile, 0)


def _ssd_body(xs_ref, bp_ref, cp_ref, dt_ref, wx_ref, wb_ref, wc_ref, bx_ref, bb_ref, bc_ref,
              alog_ref, dsk_ref, e_ref, y_ref,
              pad_ref, xc_ref, bcv_ref, ccv_ref, part_ref, stf_ref, stb_ref, w_ref, te_ref, fs_ref,
              *, seq):
    n_chunks = seq // CHUNK
    n_steps = n_chunks // SSD_CHUNKS_PER_STEP
    assert n_chunks % (4 * SSD_CHUNKS_PER_STEP) == 0
    _conv_silu(xs_ref, wx_ref, bx_ref, xc_ref, pad_ref, seq=seq, width=GROUP_WIDTH)
    _conv_silu(bp_ref, wb_ref, bb_ref, bcv_ref, pad_ref, seq=seq, width=SSD_STATE)
    _conv_silu(cp_ref, wc_ref, bc_ref, ccv_ref, pad_ref, seq=seq, width=SSD_STATE)

    row = lax.broadcasted_iota(jnp.int32, (CHUNK, CHUNK), 0)
    col = lax.broadcasted_iota(jnp.int32, (CHUNK, CHUNK), 1)
    lane = lax.broadcasted_iota(jnp.int32, (CHUNK, LANES), 1)
    neg_a = -jnp.exp(alog_ref[...])

    def chains_of(step):
        chains = []
        for u in range(SSD_CHUNKS_PER_STEP):
            c = step * SSD_CHUNKS_PER_STEP + u
            chains += [(c, 0, stf_ref), (n_chunks - 1 - c, 1, stb_ref)]
        return [(pl.multiple_of(c * CHUNK, CHUNK), d, st) for c, d, st in chains]

    def keep_mask(direction):
        return (col <= row) if direction == 0 else (col >= row)

    def running_sums(step):
        out = []
        for r0, direction, _ in chains_of(step):
            dt = dt_ref[pl.ds(r0, CHUNK), :]
            tri = jnp.where(keep_mask(direction), 1.0, 0.0).astype(BF16)
            acs2 = jnp.dot(tri, _split_bf16(neg_a * dt), preferred_element_type=F32)
            out.append((dt, acs2[:, :LANES] + acs2[:, LANES:]))
        return out

    def chunk_scalings(step, slot, sums):
        for k, ((r0, direction, _), (dt, acs)) in enumerate(zip(chains_of(step), sums)):
            edge = CHUNK - 1 if direction == 0 else 0
            to_end_dt = (jnp.exp(acs[edge:edge + 1, :] - acs) * dt).astype(BF16)
            te_ref[slot, k] = jnp.dot(to_end_dt, e_ref[direction, 0:LANES, :],
                                      preferred_element_type=F32).astype(BF16)
            fs_ref[slot, k] = jnp.dot(_split_bf16(jnp.exp(acs)), e_ref[direction],
                                      preferred_element_type=F32)

    def decay_weights(step, slot, sums):
        for k, ((r0, direction, _), (dt, acs)) in enumerate(zip(chains_of(step), sums)):
            cb = lax.dot_general(ccv_ref[pl.ds(r0, CHUNK), :], bcv_ref[pl.ds(r0, CHUNK), :],
                                 (((1,), (1,)), ((), ())), preferred_element_type=F32)
            query = acs * LOG2E
            source_t = (query - jnp.log2(dt)).T
            keep = keep_mask(direction)
            for pair in range(GROUP_HEADS // 2):
                w_pair = []
                for hh in range(2):
                    j = GROUP_HEADS * direction + 2 * pair + hh
                    seg = query[:, j:j + 1] - source_t[j:j + 1, :]
                    w_pair.append((jnp.exp2(jnp.where(keep, seg, -jnp.inf)) * cb).astype(BF16))
                w_ref[slot, k, pair] = jnp.concatenate(w_pair, axis=1)

    def within_chunks(step, slot):
        out = []
        for k, (r0, _, _) in enumerate(chains_of(step)):
            x = xc_ref[pl.ds(r0, CHUNK), :]
            y_parts = []
            for pair in range(GROUP_HEADS // 2):
                xp = x[:, pair * LANES:(pair + 1) * LANES]
                zero = jnp.zeros_like(xp)
                x_bd = jnp.concatenate([jnp.where(lane < HEAD_DIM, xp, zero),
                                        jnp.where(lane >= HEAD_DIM, xp, zero)], axis=0)
                y_parts.append(jnp.dot(w_ref[slot, k, pair], x_bd, preferred_element_type=F32))
            out.append(jnp.concatenate(y_parts, axis=1))
        return out

    def across_chunks(step, slot, y_diag, second_visit):
        for k, ((r0, direction, st_ref), yd) in enumerate(zip(chains_of(step), y_diag)):
            edge = CHUNK - 1 if direction == 0 else 0
            x = xc_ref[pl.ds(r0, CHUNK), :]
            from_start_x = fs_ref[slot, k]
            s_in = st_ref[...]
            y = yd + jnp.dot(ccv_ref[pl.ds(r0, CHUNK), :], s_in.astype(BF16),
                             preferred_element_type=F32) * from_start_x
            s_new = lax.dot_general(bcv_ref[pl.ds(r0, CHUNK), :], x * te_ref[slot, k],
                                    (((0,), (0,)), ((), ())), preferred_element_type=F32)
            st_ref[...] = s_in * from_start_x[edge:edge + 1, :] + s_new
            if direction == 0:
                y = y + dsk_ref[...] * x.astype(F32)
            if second_visit:
                y_ref[pl.ds(r0, CHUNK), :] = (part_ref[pl.ds(r0, CHUNK), :] + y).astype(y_ref.dtype)
            else:
                part_ref[pl.ds(r0, CHUNK), :] = y

    stf_ref[...] = jnp.zeros_like(stf_ref)
    stb_ref[...] = jnp.zeros_like(stb_ref)

    sums = running_sums(0)
    chunk_scalings(0, 0, sums)
    decay_weights(0, 0, sums)

    def two_steps(j, second_visit):
        for half in range(2):
            step = 2 * j + half
            nxt = jnp.minimum(step + 1, n_steps - 1)
            y_diag = within_chunks(step, half)
            sums = running_sums(nxt)
            across_chunks(step, half, y_diag, second_visit)
            chunk_scalings(nxt, 1 - half, sums)
            decay_weights(nxt, 1 - half, sums)

    def first_visits(j, carry):
        two_steps(j, False)
        return carry

    def second_visits(j, carry):
        two_steps(j, True)
        return carry

    lax.fori_loop(0, n_steps // 4, first_visits, 0)
    lax.fori_loop(n_steps // 4, n_steps // 2, second_visits, 0)


def _ssd_mixer(proj, dt, p, *, batch, seq):
    rows = batch * seq
    gw, ns = GROUP_WIDTH, SSD_STATE
    xs0 = MIX_WIDTH // gw
    b0 = (MIX_WIDTH + BRANCH) // ns
    c0 = b0 + SSD_GROUPS
    in_specs = [
        pl.BlockSpec((seq, gw), lambda b, g: (b, xs0 + g)),
        pl.BlockSpec((seq, ns), lambda b, g: (b, b0 + g)),
        pl.BlockSpec((seq, ns), lambda b, g: (b, c0 + g)),
        pl.BlockSpec((seq, LANES), lambda b, g: (b, g)),
        pl.BlockSpec((SSD_CONV, gw), lambda b, g: (0, g)),
        pl.BlockSpec((SSD_CONV, ns), lambda b, g: (0, BRANCH // ns + g)),
        pl.BlockSpec((SSD_CONV, ns), lambda b, g: (0, BRANCH // ns + SSD_GROUPS + g)),
        pl.BlockSpec((1, gw), lambda b, g: (0, g)),
        pl.BlockSpec((1, ns), lambda b, g: (0, BRANCH // ns + g)),
        pl.BlockSpec((1, ns), lambda b, g: (0, BRANCH // ns + SSD_GROUPS + g)),
        pl.BlockSpec((1, LANES), lambda b, g: (0, g)),
        pl.BlockSpec((1, gw), lambda b, g: (0, g)),
        pl.BlockSpec((2, 2 * LANES, gw), lambda b, g: (0, 0, 0)),
    ]
    return pl.pallas_call(
        functools.partial(_ssd_body, seq=seq),
        grid=(batch, SSD_GROUPS),
        in_specs=in_specs,
        out_specs=pl.BlockSpec((seq, gw), lambda b, g: (b, g)),
        out_shape=jax.ShapeDtypeStruct((rows, BRANCH), BF16),
        scratch_shapes=[
            pltpu.VMEM((seq + 2 * CONV_HALO, LANES), F32),
            pltpu.VMEM((seq, gw), BF16),
            pltpu.VMEM((seq, ns), BF16),
            pltpu.VMEM((seq, ns), BF16),
            pltpu.VMEM((seq, gw), F32),
            pltpu.VMEM((ns, gw), F32),
            pltpu.VMEM((ns, gw), F32),
            pltpu.VMEM((2, 2 * SSD_CHUNKS_PER_STEP, GROUP_HEADS // 2, CHUNK, 2 * CHUNK), BF16),
            pltpu.VMEM((2, 2 * SSD_CHUNKS_PER_STEP, CHUNK, gw), BF16),
            pltpu.VMEM((2, 2 * SSD_CHUNKS_PER_STEP, CHUNK, gw), F32),
        ],
        compiler_params=_params("parallel", "parallel"),
        name="ssd_mixer",
    )(proj, proj, proj, dt, p["conv_w"], p["conv_w"], p["conv_w"],
      p["conv_b"], p["conv_b"], p["conv_b"], p["a_log"], p["d_skip"], p["expand"])


def _na_body(q_ref, k_ref, v_ref, bias_ref, o_ref, *, seq):
    n_rows = seq // GRID_W
    kh = min(NA_KH, n_rows)
    win = kh * GRID_W
    rows_per_step = min(n_rows, NA_ROWS_PER_STEP)
    assert n_rows % rows_per_step == 0
    lane = lax.broadcasted_iota(jnp.int32, (GRID_W, LANES), 1)
    lane2 = lax.broadcasted_iota(jnp.int32, (2 * GRID_W, LANES), 1)
    row2 = lax.broadcasted_iota(jnp.int32, (2 * GRID_W, LANES), 0)
    own_channels = (lane2 >= HEAD_DIM) == (row2 >= GRID_W)

    def window(r):
        rs = jnp.clip(r - kh // 2, 0, n_rows - kh)
        return pl.multiple_of(r * GRID_W, GRID_W), pl.multiple_of(rs * GRID_W, GRID_W), r - rs

    def rows_step(i, carry):
        rows = [window(i * rows_per_step + u) for u in range(rows_per_step)]
        scores = []
        for q0, k0, _ in rows:
            q = q_ref[pl.ds(q0, GRID_W), :]
            qq = jnp.concatenate([q, q], axis=0)
            qq = jnp.where(own_channels, qq, jnp.zeros_like(qq))
            scores.append(lax.dot_general(qq, k_ref[pl.ds(k0, win), :], (((1,), (1,)), ((), ())),
                                          preferred_element_type=F32))
        probs = []
        for (_, _, off), s in zip(rows, scores):
            s = s + bias_ref[off].reshape(2 * GRID_W, win)
            p = jnp.exp2(s - jnp.max(s, axis=-1, keepdims=True))
            probs.append((p.astype(BF16), jnp.sum(p, axis=-1, keepdims=True)))
        for (q0, k0, _), (p, denom) in zip(rows, probs):
            pv = jnp.dot(p, v_ref[pl.ds(k0, win), :], preferred_element_type=F32) / denom
            o = jnp.where(lane < HEAD_DIM, pv[0:GRID_W], pv[GRID_W:2 * GRID_W])
            o_ref[pl.ds(q0, GRID_W), :] = o.astype(o_ref.dtype)
        return carry

    lax.fori_loop(0, n_rows // rows_per_step, rows_step, 0)


def _na_mixer(proj, bias_tab, *, batch, seq):
    rows = batch * seq
    q0 = MIX_WIDTH // LANES
    k0 = q0 + BRANCH // LANES
    v0 = k0 + BRANCH // LANES
    n_off, _, _, win = bias_tab.shape
    return pl.pallas_call(
        functools.partial(_na_body, seq=seq),
        grid=(NA_HEADS // 2, batch),
        in_specs=[
            pl.BlockSpec((seq, LANES), lambda hp, b: (b, q0 + hp)),
            pl.BlockSpec((seq, LANES), lambda hp, b: (b, k0 + hp)),
            pl.BlockSpec((seq, LANES), lambda hp, b: (b, v0 + hp)),
            pl.BlockSpec((n_off, 2, GRID_W, win), lambda hp, b: (0, hp, 0, 0)),
        ],
        out_specs=pl.BlockSpec((seq, LANES), lambda hp, b: (b, hp)),
        out_shape=jax.ShapeDtypeStruct((rows, BRANCH), BF16),
        compiler_params=_params("parallel", "parallel"),
        name="na_mixer",
    )(proj, proj, proj, bias_tab)

def _na_bias_table(rpb, n_rows):
    kh = min(NA_KH, n_rows)
    col = jnp.arange(GRID_W)
    col_start = jnp.clip(col - NA_KW // 2, 0, GRID_W - NA_KW)
    col_in = (col[None, :] >= col_start[:, None]) & (col[None, :] < col_start[:, None] + NA_KW)
    dc = jnp.clip(col[None, :] - col[:, None], -(NA_KW - 1), NA_KW - 1) + NA_KW - 1
    by_dc = jnp.take(rpb, dc.reshape(-1), axis=2).reshape(NA_HEADS, 2 * NA_KH - 1, GRID_W, GRID_W)
    by_dc = jnp.where(col_in[None, None], by_dc * LOG2E, -jnp.inf).transpose(0, 2, 1, 3)
    flat = by_dc.reshape(NA_HEADS, GRID_W, (2 * NA_KH - 1) * GRID_W)
    tabs = [flat[:, :, (NA_KH - 1 - off) * GRID_W:(NA_KH - 1 - off + kh) * GRID_W] for off in range(kh)]
    return jnp.stack(tabs, axis=0).astype(F32)


def _out_proj_body(y_ref, z_ref, q_ref, kv_ref, x_ref, pg_ref, w_ref, o_ref, *, gated_norm):
    heads = [slice(h * MEM_HEAD_DIM, (h + 1) * MEM_HEAD_DIM) for h in range(MEM_HEADS)]
    scores = [lax.dot_general(q_ref[:, c], kv_ref[:, c], (((1,), (1,)), ((), ())),
                              preferred_element_type=F32) for c in heads]
    yg = y_ref[...].astype(F32) * _silu(z_ref[:, 0:BRANCH].astype(F32))
    acc = jnp.dot(yg.astype(BF16), w_ref[0:BRANCH, :], preferred_element_type=F32)
    if gated_norm:
        acc = acc * lax.rsqrt(jnp.mean(yg * yg, axis=-1, keepdims=True) + EPS)
    for c, s in zip(heads, scores):
        p = jnp.exp2(s - jnp.max(s, axis=-1, keepdims=True))
        denom = jnp.sum(p, axis=-1, keepdims=True)
        vcols = slice(MEM_WIDTH + c.start, MEM_WIDTH + c.stop)
        mem = jnp.dot(p.astype(BF16), kv_ref[:, vcols], preferred_element_type=F32) / denom
        gate = _silu(z_ref[:, BRANCH + c.start:BRANCH + c.stop].astype(F32))
        acc = acc + jnp.dot((mem * gate).astype(BF16), w_ref[BRANCH + c.start:BRANCH + c.stop, :],
                            preferred_element_type=F32)
    ms = jnp.mean(acc * acc, axis=-1, keepdims=True)
    o_ref[...] = x_ref[...] + acc * lax.rsqrt(ms + EPS) * pg_ref[...]


def _out_proj(y, proj, kv, x, post_g, w_out, *, seq, q_col, gated_norm):
    rows, d = x.shape
    tm = min(seq, 512)
    per_seq = seq // tm
    return pl.pallas_call(
        functools.partial(_out_proj_body, gated_norm=gated_norm),
        grid=(rows // tm,),
        in_specs=[
            pl.BlockSpec((tm, BRANCH), lambda i: (i, 0)),
            pl.BlockSpec((tm, MIX_WIDTH), lambda i: (i, 0)),
            pl.BlockSpec((tm, MEM_WIDTH), lambda i: (i, q_col)),
            pl.BlockSpec((MEM_TOKENS, 2 * MEM_WIDTH), lambda i: (i // per_seq, 0)),
            pl.BlockSpec((tm, d), lambda i: (i, 0)),
            pl.BlockSpec((1, d), lambda i: (0, 0)),
            pl.BlockSpec((MIX_WIDTH, d), lambda i: (0, 0)),
        ],
        out_specs=pl.BlockSpec((tm, d), lambda i: (i, 0)),
        out_shape=jax.ShapeDtypeStruct((rows, d), F32),
        compiler_params=_params("parallel"),
        name="out_proj",
    )(y, proj, proj, kv, x, post_g, w_out)


def _per_group_lanes(fwd, bwd):
    n = fwd.shape[0]
    f = fwd.reshape(n, SSD_GROUPS, GROUP_HEADS)
    b = bwd.reshape(n, SSD_GROUPS, GROUP_HEADS)
    pad = jnp.zeros((n, SSD_GROUPS, LANES - 2 * GROUP_HEADS), F32)
    return jnp.concatenate([f, b, pad], axis=-1).reshape(n, 1, SSD_GROUPS * LANES)


def _prepare(pre_g, post_g, mem_g, w_mem_kv, w_out, ssd_w_in, ssd_conv_w, ssd_conv_b,
             ssd_dt_bias_f, ssd_dt_bias_b, ssd_a_log_f, ssd_a_log_b, ssd_d, ssd_norm_g,
             na_w_in, na_rpb):
    xbc_end = MIX_WIDTH + BRANCH + 2 * SSD_GROUPS * SSD_STATE
    dt_end = xbc_end + 2 * SSD_HEADS
    n_ssd = ssd_w_in.shape[0]
    dt_pad = LANES - 2 * SSD_HEADS
    w_dt = jnp.pad(ssd_w_in[:, :, xbc_end:dt_end], ((0, 0), (0, 0), (0, dt_pad)))
    dt_bias = jnp.pad(jnp.concatenate([ssd_dt_bias_f, ssd_dt_bias_b], axis=-1), ((0, 0), (0, dt_pad)))
    src = lax.broadcasted_iota(jnp.int32, (2 * LANES, SSD_GROUPS * LANES), 0) % LANES
    dst = lax.broadcasted_iota(jnp.int32, (2 * LANES, SSD_GROUPS * LANES), 1)
    dst_g, dst_l = dst // LANES, dst % LANES
    dt_place = ((dst_l < 2 * GROUP_HEADS)
                & (src == SSD_HEADS * (dst_l // GROUP_HEADS) + GROUP_HEADS * dst_g + dst_l % GROUP_HEADS))
    mem_c = MEM_HEAD_DIM ** -0.5 * LOG2E
    w_main = jnp.concatenate([ssd_w_in[:, :, :xbc_end], ssd_w_in[:, :, dt_end:] * mem_c], axis=-1)
    na_cols = jnp.arange(na_w_in.shape[-1])
    na_c = jnp.where((na_cols >= MIX_WIDTH) & (na_cols < MIX_WIDTH + BRANCH), HEAD_DIM ** -0.5 * LOG2E,
                     jnp.where(na_cols >= MIX_WIDTH + 3 * BRANCH, mem_c, 1.0)).astype(F32)
    src = lax.broadcasted_iota(jnp.int32, (2, 2 * LANES, GROUP_WIDTH), 1) % LANES
    head = lax.broadcasted_iota(jnp.int32, (2, 2 * LANES, GROUP_WIDTH), 2) // HEAD_DIM
    direction = lax.broadcasted_iota(jnp.int32, (2, 2 * LANES, GROUP_WIDTH), 0)
    expand = (src == GROUP_HEADS * direction + head).astype(BF16)
    gain = jnp.ones((w_out.shape[0], MIX_WIDTH), F32).at[0::2, :BRANCH].set(ssd_norm_g)
    return dict(
        pre_g=pre_g[:, None, :], post_g=post_g[:, None, :], mem_g=mem_g[:, None, :],
        w_mem_kv=w_mem_kv.astype(BF16), w_out=(w_out * gain[:, :, None]).astype(BF16),
        ssd_w_main=w_main.astype(BF16), ssd_w_dt=w_dt.astype(BF16),
        ssd_dt_bias=dt_bias[:, None, :], ssd_dt_place=dt_place.astype(BF16),
        ssd=dict(conv_w=ssd_conv_w, conv_b=ssd_conv_b[:, None, :],
                 a_log=_per_group_lanes(ssd_a_log_f, ssd_a_log_b),
                 d_skip=jnp.repeat(ssd_d, HEAD_DIM, axis=-1)[:, None, :], expand=expand),
        na_w_in=(na_w_in * na_c).astype(BF16), na_rpb=na_rpb,
    )


def _encoder(x, mem, p):
    batch, seq, d = x.shape
    xf = x.reshape(batch * seq, d)
    memf = mem.reshape(batch * MEM_TOKENS, d)
    bias_tabs = [_na_bias_table(p["na_rpb"][j], seq // GRID_W) for j in range(p["na_rpb"].shape[0])]
    for i in range(DEPTH):
        j = i // 2
        kv = _norm_matmul(memf, p["mem_g"][i], p["w_mem_kv"][i], name="mem_kv")
        if i % 2 == 0:
            dt_params = (p["ssd_w_dt"][j], p["ssd_dt_bias"][j], p["ssd_dt_place"])
            proj, dt = _norm_matmul(xf, p["pre_g"][i], p["ssd_w_main"][j], dt_params, name="ssd_in_proj")
            layer = {k: v[j] for k, v in p["ssd"].items() if k != "expand"}
            layer["expand"] = p["ssd"]["expand"]
            y = _ssd_mixer(proj, dt, layer, batch=batch, seq=seq)
            q_col = (MIX_WIDTH + BRANCH + 2 * SSD_GROUPS * SSD_STATE) // MEM_WIDTH
        else:
            proj = _norm_matmul(xf, p["pre_g"][i], p["na_w_in"][j], name="na_in_proj")
            y = _na_mixer(proj, bias_tabs[j], batch=batch, seq=seq)
            q_col = (MIX_WIDTH + 3 * BRANCH) // MEM_WIDTH
        xf = _out_proj(y, proj, kv, xf, p["post_g"][i], p["w_out"][i], seq=seq, q_col=q_col,
                       gated_norm=(i % 2 == 0))
    return xf.reshape(batch, seq, d)


def kernel(x_prompt, x_sample, mem_prompt, mem_sample, pre_g, post_g, mem_g, w_mem_kv, w_out, ssd_w_in, ssd_conv_w, ssd_conv_b, ssd_dt_bias_f, ssd_dt_bias_b, ssd_a_log_f, ssd_a_log_b, ssd_d, ssd_norm_g, na_w_in, na_rpb):
    p = _prepare(pre_g, post_g, mem_g, w_mem_kv, w_out, ssd_w_in, ssd_conv_w, ssd_conv_b,
                 ssd_dt_bias_f, ssd_dt_bias_b, ssd_a_log_f, ssd_a_log_b, ssd_d, ssd_norm_g,
                 na_w_in, na_rpb)
    return _encoder(x_prompt, mem_prompt, p), _encoder(x_sample, mem_sample, p)
```
